```python
import jax
import jax.numpy as jnp
from jax import lax
import numpy as np

D_MODEL = 2048
BATCH = 8
SEQ = 4096
DEPTH = 2

MIX_HALF = D_MODEL // 2
HG_DK = 128
HG_HEADS = MIX_HALF // HG_DK
HG_DV = MIX_HALF // HG_HEADS
HG_CHUNK = 64
SW_DH = 128
SW_HEADS = MIX_HALF // SW_DH
SW_KV_HEADS = 2
SW_WINDOW = 128
NSA_DH = 128
NSA_HEADS = MIX_HALF // NSA_DH
NSA_KV_HEADS = 2
NSA_CMP_LEN = 32
NSA_CMP_STRIDE = 16
NSA_CMP_HIDDEN = 128
NSA_SEL_LEN = 64
NSA_N_SEL = 8
NSA_WINDOW = 512
NSA_Q_BLOCK = 64
RET_DK = 128
RET_DV = 256
RET_HEADS = MIX_HALF // RET_DV
RET_CHUNK = 128
RET_THETA_BASE = 10000.0
MOE_GROUPS = 4
MOE_EXPERTS_PER_GROUP = 8
MOE_EXPERTS = MOE_GROUPS * MOE_EXPERTS_PER_GROUP
MOE_TOPK = 2
MOE_DFF = D_MODEL // 2
MOE_ROW_BLOCK = 128
RMS_EPS = 1e-6

AB_SIZES = (HG_HEADS * HG_DK, HG_HEADS * HG_DK, HG_HEADS * HG_DV, HG_HEADS * HG_DV,
            SW_HEADS * SW_DH, SW_KV_HEADS * SW_DH, SW_KV_HEADS * SW_DH)
AB_OUT = HG_HEADS * HG_DV + SW_HEADS * SW_DH
CD_SIZES = ((NSA_HEADS * NSA_DH,) + (NSA_KV_HEADS * NSA_DH,) * 6 + (NSA_HEADS * 3,)
            + (RET_HEADS * RET_DK, RET_HEADS * RET_DK, RET_HEADS * RET_DV, RET_HEADS * RET_DV))
CD_OUT = NSA_HEADS * NSA_DH + RET_HEADS * RET_DV

kernel_name = "hybrid_hgrn2_swa_nsa_retention_hmoe"


def rmsnorm(x, w):
    x32 = x.astype(jnp.float32)
    y = x32 * lax.rsqrt(jnp.mean(x32 * x32, axis=-1, keepdims=True) + RMS_EPS)
    return (y * w.astype(jnp.float32)).astype(x.dtype)


def head_rmsnorm(o, w):
    *lead, H, d = o.shape
    y = o * lax.rsqrt(jnp.mean(o * o, axis=-1, keepdims=True) + RMS_EPS)
    return y.reshape(*lead, H * d) * w.astype(jnp.float32)


def head_layernorm(o, w):
    *lead, H, d = o.shape
    c = o - jnp.mean(o, axis=-1, keepdims=True)
    y = c * lax.rsqrt(jnp.mean(c * c, axis=-1, keepdims=True) + RMS_EPS)
    return y.reshape(*lead, H * d) * w.astype(jnp.float32)


def split_cols(a, sizes):
    offsets, acc = [], 0
    for s in sizes[:-1]:
        acc += s
        offsets.append(acc)
    return jnp.split(a, offsets, axis=-1)


def masked_softmax(s, mask):
    s = jnp.where(mask, s, -jnp.inf)
    m = jnp.max(s, axis=-1, keepdims=True)
    m = jnp.where(jnp.isfinite(m), m, 0.0)
    e = jnp.exp(s - m)
    z = jnp.sum(e, axis=-1, keepdims=True)
    return e / jnp.where(z > 0, z, 1.0)


def hgrn2_recurrence(q, f_logit, i, lb):
    B, T, H, dk = q.shape
    dv = i.shape[-1]
    C = HG_CHUNK
    N = T // C
    f = lb + (1.0 - lb) * jax.nn.sigmoid(f_logit.astype(jnp.float32))
    log_f = jnp.log(f)
    k = 1.0 - f

    def to_chunks(a):
        return a.astype(jnp.float32).reshape(B, N, C, H, a.shape[-1]).transpose(1, 0, 2, 3, 4)

    causal = jnp.tril(jnp.ones((C, C), dtype=bool))[None, :, :, None, None]

    def step(S, inp):
        qc, kc, vc, lf = inp
        b = jnp.cumsum(lf, axis=1)
        o_inter = jnp.einsum('bthd,bhde->bthe', qc * jnp.exp(b), S)
        decay = jnp.exp(jnp.where(causal, b[:, :, None] - b[:, None, :], -jnp.inf))
        a = jnp.einsum('btshd,bshd->bhts', qc[:, :, None] * decay, kc)
        o_intra = jnp.einsum('bhts,bshe->bthe', a, vc)
        b_end = b[:, -1]
        S = jnp.exp(b_end)[..., None] * S + jnp.einsum(
            'bshd,bshe->bhde', kc * jnp.exp(b_end[:, None] - b), vc)
        return S, o_inter + o_intra

    S0 = jnp.zeros((B, H, dk, dv), jnp.float32)
    _, o = lax.scan(step, S0, (to_chunks(q), to_chunks(k), to_chunks(i), to_chunks(log_f)))
    return o.transpose(1, 0, 2, 3, 4).reshape(B, T, H, dv)


def sliding_window_sink_attention(q, k, v, sinks):
    B, T, Hq, d = q.shape
    G = k.shape[2]
    R = Hq // G
    W = SW_WINDOW
    nb = T // W
    qb = q.reshape(B, nb, W, G, R, d)

    def band(a):
        ap = jnp.pad(a, ((0, 0), (W, 0), (0, 0), (0, 0))).reshape(B, nb + 1, W, G, d)
        return jnp.concatenate([ap[:, :-1], ap[:, 1:]], axis=2)

    kb, vb = band(k), band(v)
    s = jnp.einsum('bnqgrd,bnkgd->bngrqk', qb, kb, preferred_element_type=jnp.float32) * (d ** -0.5)
    rel = jnp.arange(W)[:, None] + W - jnp.arange(2 * W)[None, :]
    kpos = (jnp.arange(nb) * W - W)[:, None] + jnp.arange(2 * W)[None, :]
    valid = ((rel >= 0) & (rel < W))[None] & (kpos >= 0)[:, None, :]
    s = jnp.where(valid[None, :, None, None], s, -jnp.inf)
    sink = sinks.astype(jnp.float32).reshape(G, R)[None, None, :, :, None, None]
    m = jnp.maximum(jnp.max(s, axis=-1, keepdims=True), sink)
    p = jnp.exp(s - m)
    p = p / (jnp.sum(p, axis=-1, keepdims=True) + jnp.exp(sink - m))
    o = jnp.einsum('bngrqk,bnkgd->bnqgrd', p.astype(v.dtype), vb)
    return o.reshape(B, T, Hq * d)


def nsa_compress(a, pe, w1, w2):
    T = a.shape[1]
    nc = (T - NSA_CMP_LEN) // NSA_CMP_STRIDE + 1
    idx = jnp.arange(nc)[:, None] * NSA_CMP_STRIDE + jnp.arange(NSA_CMP_LEN)[None, :]
    blk = a[:, idx] + pe[None, None, :, None, :]
    hid = jax.nn.gelu(jnp.einsum('bnlgd,ldh->bngh', blk, w1))
    return jnp.einsum('bngh,he->bnge', hid, w2)


def nsa_attention(q, k_cmp, v_cmp, k_slc, v_slc, k_win, v_win, gates):
    B, T, H, d = q.shape
    G = k_slc.shape[2]
    R = H // G
    scale = d ** -0.5
    nc = k_cmp.shape[1]
    n_blk = T // NSA_SEL_LEN
    n_sel = min(NSA_N_SEL, n_blk)
    QB = NSA_Q_BLOCK
    W = NSA_WINDOW
    cmp_start = jnp.arange(nc) * NSA_CMP_STRIDE
    cmp_end = cmp_start + NSA_CMP_LEN - 1
    sel_start = jnp.arange(n_blk) * NSA_SEL_LEN
    overlap = ((cmp_start[:, None] < sel_start[None, :] + NSA_SEL_LEN)
               & (cmp_start[:, None] + NSA_CMP_LEN > sel_start[None, :])).astype(jnp.float32)
    k_sg = k_slc.transpose(0, 2, 1, 3)
    v_sg = v_slc.transpose(0, 2, 1, 3)
    k_wp = jnp.pad(k_win, ((0, 0), (W, 0), (0, 0), (0, 0)))
    v_wp = jnp.pad(v_win, ((0, 0), (W, 0), (0, 0), (0, 0)))
    gather = jax.vmap(jax.vmap(lambda a, idx: a[idx]))
    blk_ids = jnp.arange(n_blk)

    def query_block(start):
        t = start + jnp.arange(QB)
        qb = lax.dynamic_slice_in_dim(q, start, QB, axis=1).reshape(B, QB, G, R, d)
        gb = lax.dynamic_slice_in_dim(gates, start, QB, axis=1)
        s1 = jnp.einsum('bqgrd,bngd->bgrqn', qb, k_cmp, preferred_element_type=jnp.float32) * scale
        p1 = masked_softmax(s1, cmp_end[None, :] <= t[:, None])
        o_cmp = jnp.einsum('bgrqn,bngd->bqgrd', p1, v_cmp.astype(jnp.float32))
        imp = jnp.einsum('bgrqn,nj->bgqj', p1, overlap)
        cur = t // NSA_SEL_LEN
        forced = (blk_ids[None] == 0) | (blk_ids[None] == cur[:, None]) | (blk_ids[None] == cur[:, None] - 1)
        imp = jnp.where(blk_ids[None] > cur[:, None], -jnp.inf, imp)
        imp = jnp.where(forced, jnp.inf, imp)
        _, sel = lax.top_k(imp, n_sel)
        pos = (sel[..., None] * NSA_SEL_LEN + jnp.arange(NSA_SEL_LEN)).reshape(B, G, QB, n_sel * NSA_SEL_LEN)
        ks = gather(k_sg, pos)
        vs = gather(v_sg, pos)
        s2 = jnp.einsum('bqgrd,bgqpd->bgrqp', qb, ks, preferred_element_type=jnp.float32) * scale
        p2 = masked_softmax(s2, (pos <= t[None, None, :, None])[:, :, None])
        o_slc = jnp.einsum('bgrqp,bgqpd->bqgrd', p2, vs.astype(jnp.float32))
        kw = lax.dynamic_slice_in_dim(k_wp, start, W + QB, axis=1)
        vw = lax.dynamic_slice_in_dim(v_wp, start, W + QB, axis=1)
        kpos = start - W + jnp.arange(W + QB)
        rel = t[:, None] - kpos[None, :]
        s3 = jnp.einsum('bqgrd,bkgd->bgrqk', qb, kw, preferred_element_type=jnp.float32) * scale
        p3 = masked_softmax(s3, (rel >= 0) & (rel < W) & (kpos[None, :] >= 0))
        o_win = jnp.einsum('bgrqk,bkgd->bqgrd', p3, vw.astype(jnp.float32))
        g = jax.nn.sigmoid(gb.astype(jnp.float32)).reshape(B, QB, G, R, 3)
        o = g[..., 0:1] * o_cmp + g[..., 1:2] * o_slc + g[..., 2:3] * o_win
        return o.reshape(B, QB, H, d)

    out = lax.map(query_block, jnp.arange(T // QB, dtype=jnp.int32) * QB)
    return out.transpose(1, 0, 2, 3, 4).reshape(B, T, H * d)


def rotate(x, pos):
    half = x.shape[-1] // 2
    theta = 1.0 / (RET_THETA_BASE ** jnp.linspace(0.0, 1.0, half, dtype=jnp.float32))
    ang = pos.astype(jnp.float32)[:, None] * theta[None, :]
    cos = jnp.cos(ang)[None, :, None, :]
    sin = jnp.sin(ang)[None, :, None, :]
    x1, x2 = x[..., :half], x[..., half:]
    return jnp.concatenate([x1 * cos - x2 * sin, x2 * cos + x1 * sin], axis=-1)


def retention_chunkwise(q, k, v):
    B, T, H, dk = q.shape
    dv = v.shape[-1]
    C = RET_CHUNK
    N = T // C
    log_g = jnp.log1p(-jnp.exp2(-5.0 - jnp.arange(H, dtype=jnp.float32)))
    pos = jnp.arange(T)
    qc = rotate(q.astype(jnp.float32), pos).reshape(B, N, C, H, dk)
    kc = (rotate(k.astype(jnp.float32), pos) * (dk ** -0.5)).reshape(B, N, C, H, dk)
    vc = v.astype(jnp.float32).reshape(B, N, C, H, dv)
    j = jnp.arange(C, dtype=jnp.float32)
    diff = j[:, None] - j[None, :]
    dmat = jnp.where(diff >= 0, jnp.exp(jnp.maximum(diff, 0.0)[None] * log_g[:, None, None]), 0.0)
    inner = jnp.einsum('bnqhd,bnkhd->bnhqk', qc, kc) * dmat[None, None]
    o_inner = jnp.einsum('bnhqk,bnkhe->bnqhe', inner, vc)
    zeta = jnp.exp((C - 1 - j)[:, None] * log_g[None, :])
    kv = jnp.einsum('bnkhd,bnkhe->nbhde', kc * zeta[:, :, None], vc)
    chunk_decay = jnp.exp(C * log_g)[None, :, None, None]

    def step(state, kv_n):
        return chunk_decay * state + kv_n, state

    _, r_prev = lax.scan(step, jnp.zeros((B, H, dk, dv), jnp.float32), kv)
    xi = jnp.exp((j + 1.0)[:, None] * log_g[None, :])
    o_cross = jnp.einsum('bnqhd,nbhde->bnqhe', qc * xi[:, :, None], r_prev)
    return (o_inner + o_cross).reshape(B, T, H, dv)


def mixer_ab(h, w_in, w_out, lb, hg_norm, sw_sinks):
    B, T, _ = h.shape
    hq, hf, hi, hg, sq, sk, sv = split_cols(h @ w_in, AB_SIZES)
    o_a = hgrn2_recurrence(hq.reshape(B, T, HG_HEADS, HG_DK), hf.reshape(B, T, HG_HEADS, HG_DK),
                           hi.reshape(B, T, HG_HEADS, HG_DV), lb.reshape(HG_HEADS, HG_DK))
    o_a = head_rmsnorm(o_a, hg_norm) * jax.nn.silu(hg.astype(jnp.float32))
    o_b = sliding_window_sink_attention(sq.reshape(B, T, SW_HEADS, SW_DH), sk.reshape(B, T, SW_KV_HEADS, SW_DH),
                                        sv.reshape(B, T, SW_KV_HEADS, SW_DH), sw_sinks)
    return jnp.concatenate([o_a.astype(h.dtype), o_b.astype(h.dtype)], axis=-1) @ w_out


def mixer_cd(h, w_in, w_out, pe_k, w1_k, w2_k, pe_v, w1_v, w2_v, ret_norm):
    B, T, _ = h.shape
    nq, nkc, nvc, nks, nvs, nkw, nvw, ngt, rq, rk, rv, rg = split_cols(h @ w_in, CD_SIZES)
    G = NSA_KV_HEADS
    kv_heads = lambda a: a.reshape(B, T, G, NSA_DH)
    k_cmp = nsa_compress(kv_heads(nkc), pe_k, w1_k, w2_k)
    v_cmp = nsa_compress(kv_heads(nvc), pe_v, w1_v, w2_v)
    o_c = nsa_attention(nq.reshape(B, T, NSA_HEADS, NSA_DH), k_cmp, v_cmp, kv_heads(nks), kv_heads(nvs),
                        kv_heads(nkw), kv_heads(nvw), ngt.reshape(B, T, NSA_HEADS, 3))
    o_d = retention_chunkwise(rq.reshape(B, T, RET_HEADS, RET_DK), rk.reshape(B, T, RET_HEADS, RET_DK),
                              rv.reshape(B, T, RET_HEADS, RET_DV))
    o_d = head_layernorm(o_d, ret_norm) * jax.nn.silu(rg.astype(jnp.float32))
    return jnp.concatenate([o_c.astype(h.dtype), o_d.astype(h.dtype)], axis=-1) @ w_out


def routed_experts(xf, expert, gate, w_gate, w_up, w_down):
    N, K = expert.shape
    E = w_gate.shape[0]
    D = xf.shape[1]
    BLK = MOE_ROW_BLOCK
    A = N * K
    flat_e = expert.reshape(A)
    order = jnp.argsort(flat_e)
    e_sorted = flat_e[order]
    counts = jnp.bincount(flat_e, length=E)
    padded = (counts + BLK - 1) // BLK * BLK
    pad_end = jnp.cumsum(padded)
    pad_start = pad_end - padded
    start = jnp.cumsum(counts) - counts
    dest = pad_start[e_sorted] + jnp.arange(A) - start[e_sorted]
    rows = A + E * BLK
    n_blocks = rows // BLK
    row_tok = jnp.zeros((rows,), jnp.int32).at[dest].set((order // K).astype(jnp.int32))
    row_gate = jnp.zeros((rows,), jnp.float32).at[dest].set(gate.reshape(A)[order].astype(jnp.float32))
    blk_expert = jnp.minimum(jnp.searchsorted(pad_end, jnp.arange(n_blocks) * BLK, side='right'), E - 1)
    xs = xf[row_tok].reshape(n_blocks, BLK, D)

    def expert_block(args):
        xb, e = args
        hb = jax.nn.silu(xb @ w_gate[e]) * (xb @ w_up[e])
        return hb @ w_down[e]

    ys = lax.map(expert_block, (xs, blk_expert)).reshape(rows, D)
    out = jnp.zeros((N, D), jnp.float32).at[row_tok].add(ys.astype(jnp.float32) * row_gate[:, None])
    return out.astype(xf.dtype)


def hier_moe(x, w_grp, b_grp, w_exp, b_exp, w_gate, w_up, w_down):
    B, T, D = x.shape
    N = B * T
    xf = x.reshape(N, D)
    grp_logits = jnp.einsum('nd,dg->ng', xf, w_grp, preferred_element_type=jnp.float32) + b_grp.astype(jnp.float32)
    grp_prob = jax.nn.softmax(grp_logits, axis=-1)
    g_idx = jnp.argmax(grp_logits, axis=-1)
    p_grp = jnp.take_along_axis(grp_prob, g_idx[:, None], axis=1)[:, 0]
    exp_logits = (jnp.einsum('nd,de->ne', xf, w_exp, preferred_element_type=jnp.float32)
                  + b_exp.astype(jnp.float32)).reshape(N, MOE_GROUPS, MOE_EXPERTS_PER_GROUP)
    within = jnp.take_along_axis(exp_logits, g_idx[:, None, None], axis=1)[:, 0]
    top_p, top_i = lax.top_k(jax.nn.softmax(within, axis=-1), MOE_TOPK)
    gate = p_grp[:, None] * top_p / jnp.sum(top_p, axis=-1, keepdims=True)
    expert = (g_idx[:, None] * MOE_EXPERTS_PER_GROUP + top_i).astype(jnp.int32)
    return routed_experts(xf, expert, gate, w_gate, w_up, w_down).reshape(B, T, D)


def setup_inputs(seed: int = 0) -> dict:
    key = jax.random.key(seed)
    keys = iter(jax.random.split(key, 32))

    def normal(shape, scale):
        return jax.random.normal(next(keys), shape, jnp.float32) * scale

    n_even = (DEPTH + 1) // 2
    n_odd = DEPTH // 2
    D = D_MODEL
    return {
        "x": normal((BATCH, SEQ, D), 1.0),
        "norm_mix": 1.0 + normal((DEPTH, D), 0.05),
        "norm_ffn": 1.0 + normal((DEPTH, D), 0.05),
        "norm_final": 1.0 + normal((D,), 0.05),
        "ab_w_in": normal((n_even, D, sum(AB_SIZES)), D ** -0.5),
        "ab_w_out": normal((n_even, AB_OUT, D), AB_OUT ** -0.5),
        "hg_lb_logits": normal((DEPTH + 1, HG_HEADS * HG_DK), 0.5),
        "hg_norm": 1.0 + normal((n_even, HG_HEADS * HG_DV), 0.05),
        "sw_sinks": normal((n_even, SW_HEADS), 0.5),
        "cd_w_in": normal((n_odd, D, sum(CD_SIZES)), D ** -0.5),
        "cd_w_out": normal((n_odd, CD_OUT, D), CD_OUT ** -0.5),
        "nsa_pe_k": normal((n_odd, NSA_CMP_LEN, NSA_DH), 0.1),
        "nsa_w1_k": normal((n_odd, NSA_CMP_LEN, NSA_DH, NSA_CMP_HIDDEN), (NSA_CMP_LEN * NSA_DH) ** -0.5),
        "nsa_w2_k": normal((n_odd, NSA_CMP_HIDDEN, NSA_DH), NSA_CMP_HIDDEN ** -0.5),
        "nsa_pe_v": normal((n_odd, NSA_CMP_LEN, NSA_DH), 0.1),
        "nsa_w1_v": normal((n_odd, NSA_CMP_LEN, NSA_DH, NSA_CMP_HIDDEN), (NSA_CMP_LEN * NSA_DH) ** -0.5),
        "nsa_w2_v": normal((n_odd, NSA_CMP_HIDDEN, NSA_DH), NSA_CMP_HIDDEN ** -0.5),
        "ret_norm": 1.0 + normal((n_odd, RET_HEADS * RET_DV), 0.05),
        "moe_w_grp": normal((DEPTH, D, MOE_GROUPS), D ** -0.5),
        "moe_b_grp": normal((DEPTH, MOE_GROUPS), 0.01),
        "moe_w_exp": normal((DEPTH, D, MOE_EXPERTS), D ** -0.5),
        "moe_b_exp": normal((DEPTH, MOE_EXPERTS), 0.01),
        "moe_w_gate": normal((DEPTH, MOE_EXPERTS, D, MOE_DFF), D ** -0.5),
        "moe_w_up": normal((DEPTH, MOE_EXPERTS, D, MOE_DFF), D ** -0.5),
        "moe_w_down": normal((DEPTH, MOE_EXPERTS, MOE_DFF, D), MOE_DFF ** -0.5),
    }


def reference(x, norm_mix, norm_ffn, norm_final, ab_w_in, ab_w_out, hg_lb_logits, hg_norm, sw_sinks,
              cd_w_in, cd_w_out, nsa_pe_k, nsa_w1_k, nsa_w2_k, nsa_pe_v, nsa_w1_v, nsa_w2_v, ret_norm,
              moe_w_grp, moe_b_grp, moe_w_exp, moe_b_exp, moe_w_gate, moe_w_up, moe_w_down):
    lb_all = jnp.cumsum(jax.nn.softmax(hg_lb_logits.astype(jnp.float32), axis=0), axis=0)
    h = x
    for layer in range(DEPTH):
        hn = rmsnorm(h, norm_mix[layer])
        if layer % 2 == 0:
            e = layer // 2
            h = h + mixer_ab(hn, ab_w_in[e], ab_w_out[e], lb_all[layer], hg_norm[e], sw_sinks[e])
        else:
            o = layer // 2
            h = h + mixer_cd(hn, cd_w_in[o], cd_w_out[o], nsa_pe_k[o], nsa_w1_k[o], nsa_w2_k[o],
                             nsa_pe_v[o], nsa_w1_v[o], nsa_w2_v[o], ret_norm[o])
        hn = rmsnorm(h, norm_ffn[layer])
        h = h + hier_moe(hn, moe_w_grp[layer], moe_b_grp[layer], moe_w_exp[layer], moe_b_exp[layer],
                         moe_w_gate[layer], moe_w_up[layer], moe_w_down[layer])
    return rmsnorm(h, norm_final)
```

```python
import functools

import jax
import jax.numpy as jnp
from jax import lax
from jax.experimental import pallas as pl
from jax.experimental.pallas import tpu as pltpu

F32 = jnp.float32
BF16 = jnp.bfloat16

D_MODEL = 2048
MIX_HALF = D_MODEL // 2
RMS_EPS = 1e-6
LANES = 128
VMEM_LIMIT = 56 * 1024 * 1024

HG_DK = 128
HG_HEADS = 8
HG_CHUNK = 64
HG_SUB = 16
SW_DH = 128
SW_HEADS = 8
SW_KV = 2
SW_WINDOW = 128
NSA_DH = 128
NSA_HEADS = 8
NSA_KV = 2
NSA_CMP_LEN = 32
NSA_CMP_STRIDE = 16
NSA_CMP_HIDDEN = 128
NSA_SEL_LEN = 64
NSA_N_SEL = 8
NSA_WINDOW = 512
RET_DK = 128
RET_DV = 256
RET_HEADS = 4
RET_CHUNK = 128
RET_THETA_BASE = 10000.0
MOE_GROUPS = 4
MOE_EPG = 8
MOE_EXPERTS = 32
MOE_TOPK = 2
MOE_DFF = 1024
MOE_TM = 256

AB_N = 5632
CD_N = 5760
CD_GATE_COL = 5632

NT_DIMS = (((1,), (1,)), ((), ()))
TN_DIMS = (((0,), (0,)), ((), ()))


def _params(sem, vmem=VMEM_LIMIT):
    return pltpu.CompilerParams(dimension_semantics=sem, vmem_limit_bytes=vmem)


def _sigmoid(x):
    return 1.0 / (1.0 + jnp.exp(-x))


def _silu(x):
    return x * _sigmoid(x)


def _norm_matmul_kernel(x_ref, nw_ref, w_ref, o_ref, xn_ref):
    @pl.when(pl.program_id(1) == 0)
    def _():
        x = x_ref[...]
        ms = jnp.mean(x * x, axis=-1, keepdims=True)
        xn_ref[...] = (x * lax.rsqrt(ms + RMS_EPS) * nw_ref[...]).astype(BF16)

    o_ref[...] = jnp.dot(xn_ref[...], w_ref[...], preferred_element_type=F32)


def norm_matmul(x, nw, w, tm, tn):
    m, d = x.shape
    n = w.shape[1]
    return pl.pallas_call(
        _norm_matmul_kernel,
        out_shape=jax.ShapeDtypeStruct((m, n), F32),
        grid=(m // tm, n // tn),
        in_specs=[
            pl.BlockSpec((tm, d), lambda i, j: (i, 0)),
            pl.BlockSpec((1, d), lambda i, j: (0, 0)),
            pl.BlockSpec((d, tn), lambda i, j: (0, j)),
        ],
        out_specs=pl.BlockSpec((tm, tn), lambda i, j: (i, j)),
        scratch_shapes=[pltpu.VMEM((tm, d), BF16)],
        compiler_params=_params(("parallel", "arbitrary")),
        name="norm_matmul",
    )(x, nw.reshape(1, d), w)


def _outproj_kernel(a_ref, b_ref, w_ref, h_ref, o_ref):
    half = a_ref.shape[1]
    acc = jnp.dot(a_ref[...], w_ref[0:half, :], preferred_element_type=F32)
    acc = acc + jnp.dot(b_ref[...], w_ref[half:, :], preferred_element_type=F32)
    o_ref[...] = h_ref[...] + acc


def outproj_residual(a, b, w, h, tm):
    m, half = a.shape
    d = w.shape[1]
    return pl.pallas_call(
        _outproj_kernel,
        out_shape=jax.ShapeDtypeStruct((m, d), F32),
        grid=(m // tm,),
        in_specs=[
            pl.BlockSpec((tm, half), lambda i: (i, 0)),
            pl.BlockSpec((tm, half), lambda i: (i, 0)),
            pl.BlockSpec((2 * half, d), lambda i: (0, 0)),
            pl.BlockSpec((tm, d), lambda i: (i, 0)),
        ],
        out_specs=pl.BlockSpec((tm, d), lambda i: (i, 0)),
        compiler_params=_params(("parallel",)),
        name="outproj_residual",
    )(a, b, w, h)


def _hgrn2_kernel(q_ref, f_ref, i_ref, g_ref, lb_ref, nw_ref, o_ref, st_ref, *, n_chunks):
    C, c = HG_CHUNK, HG_SUB
    nsub = C // c

    @pl.when(pl.program_id(2) == 0)
    def _():
        st_ref[...] = jnp.zeros_like(st_ref)

    lb = lb_ref[...]
    nw = nw_ref[...]
    tri = (lax.broadcasted_iota(jnp.int32, (C, C), 0)
           >= lax.broadcasted_iota(jnp.int32, (C, C), 1)).astype(F32)
    tloc = lax.broadcasted_iota(jnp.int32, (1, c, 1), 1)

    def chunk(ci, carry):
        r0 = pl.multiple_of(ci * C, C)
        q = q_ref[pl.ds(r0, C), :]
        v = i_ref[pl.ds(r0, C), :]
        g = g_ref[pl.ds(r0, C), :]
        f = lb + (1.0 - lb) * _sigmoid(f_ref[pl.ds(r0, C), :])
        lf = jnp.log(f)
        k = 1.0 - f
        b = jnp.dot(tri, lf, preferred_element_type=F32, precision=lax.Precision.HIGHEST)
        st = st_ref[...]
        vb = v.astype(BF16)
        o = lax.dot_general((q * jnp.exp(b)).astype(BF16), st.astype(BF16), NT_DIMS,
                            preferred_element_type=F32)
        rows = [o[0:c]]
        for i in range(1, nsub):
            r = b[i * c - 1:i * c, :]
            qi = q[i * c:(i + 1) * c] * jnp.exp(b[i * c:(i + 1) * c] - r)
            ki = k[0:i * c] * jnp.exp(r - b[0:i * c])
            a = lax.dot_general(qi.astype(BF16), ki.astype(BF16), NT_DIMS,
                                preferred_element_type=F32)
            rows.append(o[i * c:(i + 1) * c]
                        + jnp.dot(a.astype(BF16), vb[0:i * c], preferred_element_type=F32))
        o = jnp.concatenate(rows, axis=0)
        b4 = b.reshape(nsub, c, HG_DK)
        q4 = q.reshape(nsub, c, HG_DK)
        k4 = k.reshape(nsub, c, HG_DK)
        v4 = v.reshape(nsub, c, HG_DK)
        od = jnp.zeros((nsub, c, HG_DK), F32)
        for s in range(c):
            e = jnp.exp(jnp.minimum(b4 - b4[:, s:s + 1, :], 0.0))
            a = jnp.sum(q4 * e * k4[:, s:s + 1, :], axis=-1, keepdims=True)
            a = jnp.where(tloc >= s, a, 0.0)
            od = od + a * v4[:, s:s + 1, :]
        o = o + od.reshape(C, HG_DK)
        bend = b[C - 1:C, :]
        kd = k * jnp.exp(bend - b)
        st_ref[...] = st * jnp.exp(bend) + lax.dot_general(
            vb, kd.astype(BF16), TN_DIMS, preferred_element_type=F32)
        y = o * lax.rsqrt(jnp.mean(o * o, axis=-1, keepdims=True) + RMS_EPS) * nw
        o_ref[pl.ds(r0, C), :] = (y * _silu(g)).astype(o_ref.dtype)
        return carry

    lax.fori_loop(0, n_chunks, chunk, 0)


def hgrn2(y, lb, nw, batch, seq, tc):
    m = y.shape[0]
    nt = seq // tc
    row = lambda b, h, t: b * nt + t
    hd = HG_HEADS
    return pl.pallas_call(
        functools.partial(_hgrn2_kernel, n_chunks=tc // HG_CHUNK),
        out_shape=jax.ShapeDtypeStruct((m, MIX_HALF), BF16),
        grid=(batch, hd, nt),
        in_specs=[
            pl.BlockSpec((tc, HG_DK), lambda b, h, t: (row(b, h, t), h)),
            pl.BlockSpec((tc, HG_DK), lambda b, h, t: (row(b, h, t), hd + h)),
            pl.BlockSpec((tc, HG_DK), lambda b, h, t: (row(b, h, t), 2 * hd + h)),
            pl.BlockSpec((tc, HG_DK), lambda b, h, t: (row(b, h, t), 3 * hd + h)),
            pl.BlockSpec((None, 1, HG_DK), lambda b, h, t: (h, 0, 0)),
            pl.BlockSpec((None, 1, HG_DK), lambda b, h, t: (h, 0, 0)),
        ],
        out_specs=pl.BlockSpec((tc, HG_DK), lambda b, h, t: (row(b, h, t), h)),
        scratch_shapes=[pltpu.VMEM((HG_DK, HG_DK), F32)],
        compiler_params=_params(("parallel", "parallel", "arbitrary")),
        name="hgrn2",
    )(y, y, y, y, lb.reshape(hd, 1, HG_DK), nw.reshape(hd, 1, HG_DK))


def _swa_kernel(q_ref, kp_ref, kc_ref, vp_ref, vc_ref, sink_ref, o_ref):
    W = SW_WINDOW
    R = SW_HEADS // SW_KV
    n = pl.program_id(2)
    q = q_ref[...]
    qs = jnp.concatenate([q[:, r * SW_DH:(r + 1) * SW_DH] for r in range(R)], axis=0).astype(BF16)
    kk = jnp.concatenate([kp_ref[...], kc_ref[...]], axis=0).astype(BF16)
    vv = jnp.concatenate([vp_ref[...], vc_ref[...]], axis=0).astype(BF16)
    s = lax.dot_general(qs, kk, NT_DIMS, preferred_element_type=F32) * (SW_DH ** -0.5)
    s = s.reshape(R, W, 2 * W)
    tq = lax.broadcasted_iota(jnp.int32, (1, W, 2 * W), 1)
    kj = lax.broadcasted_iota(jnp.int32, (1, W, 2 * W), 2)
    rel = tq + W - kj
    valid = (rel >= 0) & (rel < W) & ((kj >= W) | (n > 0))
    s = jnp.where(valid, s, -jnp.inf)
    sink = sink_ref[...].reshape(R, W, 1)
    mx = jnp.maximum(jnp.max(s, axis=-1, keepdims=True), sink)
    p = jnp.exp(s - mx)
    z = jnp.sum(p, axis=-1, keepdims=True) + jnp.exp(sink - mx)
    o = jnp.dot(p.reshape(R * W, 2 * W).astype(BF16), vv, preferred_element_type=F32)
    o = o / z.reshape(R * W, 1)
    for r in range(R):
        o_ref[:, r * SW_DH:(r + 1) * SW_DH] = o[r * W:(r + 1) * W].astype(o_ref.dtype)


def swa(y, sinks, batch, seq):
    m = y.shape[0]
    W = SW_WINDOW
    R = SW_HEADS // SW_KV
    nb = seq // W
    qcol = 4 * MIX_HALF // (R * SW_DH)
    kcol = (4 * MIX_HALF + SW_HEADS * SW_DH) // SW_DH
    vcol = kcol + SW_KV
    cur = lambda b, g, n: b * nb + n
    prev = lambda b, g, n: b * nb + jnp.maximum(n - 1, 0)
    sink_rows = jnp.repeat(sinks.astype(F32).reshape(SW_KV, R), W, axis=1).reshape(SW_KV, R * W, 1)
    return pl.pallas_call(
        _swa_kernel,
        out_shape=jax.ShapeDtypeStruct((m, MIX_HALF), BF16),
        grid=(batch, SW_KV, nb),
        in_specs=[
            pl.BlockSpec((W, R * SW_DH), lambda b, g, n: (cur(b, g, n), qcol + g)),
            pl.BlockSpec((W, SW_DH), lambda b, g, n: (prev(b, g, n), kcol + g)),
            pl.BlockSpec((W, SW_DH), lambda b, g, n: (cur(b, g, n), kcol + g)),
            pl.BlockSpec((W, SW_DH), lambda b, g, n: (prev(b, g, n), vcol + g)),
            pl.BlockSpec((W, SW_DH), lambda b, g, n: (cur(b, g, n), vcol + g)),
            pl.BlockSpec((None, R * W, 1), lambda b, g, n: (g, 0, 0)),
        ],
        out_specs=pl.BlockSpec((W, R * SW_DH), lambda b, g, n: (cur(b, g, n), g)),
        compiler_params=_params(("parallel", "parallel", "parallel")),
        name="swa",
    )(y, y, y, y, y, sink_rows)


def _retention_kernel(q_ref, k_ref, v_ref, g_ref, cos_ref, sin_ref, dmat_ref, xi_ref, zeta_ref,
                      cdec_ref, nw_ref, o_ref, st_ref, *, n_chunks):
    C = RET_CHUNK

    @pl.when(pl.program_id(2) == 0)
    def _():
        st_ref[...] = jnp.zeros_like(st_ref)

    dmat = dmat_ref[...]
    xi = xi_ref[...]
    zeta = zeta_ref[...]
    cdec = cdec_ref[...]
    nw = nw_ref[...]

    def chunk(ci, carry):
        r0 = pl.multiple_of(ci * C, C)
        cos2 = cos_ref[pl.ds(r0, C), :]
        sin2 = sin_ref[pl.ds(r0, C), :]
        q = q_ref[pl.ds(r0, C), :]
        k = k_ref[pl.ds(r0, C), :]
        qr = q * cos2 + pltpu.roll(q, RET_DK // 2, 1) * sin2
        kr = (k * cos2 + pltpu.roll(k, RET_DK // 2, 1) * sin2) * (RET_DK ** -0.5)
        vb = v_ref[pl.ds(r0, C), :].astype(BF16)
        st = st_ref[...]
        inner = lax.dot_general(qr.astype(BF16), kr.astype(BF16), NT_DIMS,
                                preferred_element_type=F32) * dmat
        o = jnp.dot(inner.astype(BF16), vb, preferred_element_type=F32)
        o = o + jnp.dot((qr * xi).astype(BF16), st.astype(BF16), preferred_element_type=F32)
        st_ref[...] = cdec * st + lax.dot_general((kr * zeta).astype(BF16), vb, TN_DIMS,
                                                  preferred_element_type=F32)
        cen = o - jnp.mean(o, axis=-1, keepdims=True)
        y = cen * lax.rsqrt(jnp.mean(cen * cen, axis=-1, keepdims=True) + RMS_EPS) * nw
        o_ref[pl.ds(r0, C), :] = (y * _silu(g_ref[pl.ds(r0, C), :])).astype(o_ref.dtype)
        return carry

    lax.fori_loop(0, n_chunks, chunk, 0)


def retention(y, nw, batch, seq, tc):
    m = y.shape[0]
    nt = seq // tc
    H, C = RET_HEADS, RET_CHUNK
    log_g = jnp.log1p(-jnp.exp2(-5.0 - jnp.arange(H, dtype=F32)))
    j = jnp.arange(C, dtype=F32)
    diff = j[:, None] - j[None, :]
    dmat = jnp.where(diff >= 0, jnp.exp(jnp.maximum(diff, 0.0)[None] * log_g[:, None, None]), 0.0)
    zeta = jnp.exp((C - 1 - j)[None, :] * log_g[:, None]).reshape(H, C, 1)
    xi = jnp.exp((j + 1.0)[None, :] * log_g[:, None]).reshape(H, C, 1)
    cdec = jnp.broadcast_to(jnp.exp(C * log_g)[:, None, None], (H, 1, RET_DV))
    half = RET_DK // 2
    theta = 1.0 / (RET_THETA_BASE ** jnp.linspace(0.0, 1.0, half, dtype=F32))
    ang = jnp.arange(seq).astype(F32)[:, None] * theta[None, :]
    cos2 = jnp.concatenate([jnp.cos(ang), jnp.cos(ang)], axis=-1)
    sin2 = jnp.concatenate([-jnp.sin(ang), jnp.sin(ang)], axis=-1)

    row = lambda b, h, t: b * nt + t
    qcol = 2560 // RET_DK
    kcol = 3072 // RET_DK
    vcol = 3584 // RET_DV
    gcol = 4608 // RET_DV
    return pl.pallas_call(
        functools.partial(_retention_kernel, n_chunks=tc // C),
        out_shape=jax.ShapeDtypeStruct((m, MIX_HALF), BF16),
        grid=(batch, H, nt),
        in_specs=[
            pl.BlockSpec((tc, RET_DK), lambda b, h, t: (row(b, h, t), qcol + h)),
            pl.BlockSpec((tc, RET_DK), lambda b, h, t: (row(b, h, t), kcol + h)),
            pl.BlockSpec((tc, RET_DV), lambda b, h, t: (row(b, h, t), vcol + h)),
            pl.BlockSpec((tc, RET_DV), lambda b, h, t: (row(b, h, t), gcol + h)),
            pl.BlockSpec((tc, RET_DK), lambda b, h, t: (t, 0)),
            pl.BlockSpec((tc, RET_DK), lambda b, h, t: (t, 0)),
            pl.BlockSpec((None, C, C), lambda b, h, t: (h, 0, 0)),
            pl.BlockSpec((None, C, 1), lambda b, h, t: (h, 0, 0)),
            pl.BlockSpec((None, C, 1), lambda b, h, t: (h, 0, 0)),
            pl.BlockSpec((None, 1, RET_DV), lambda b, h, t: (h, 0, 0)),
            pl.BlockSpec((None, 1, RET_DV), lambda b, h, t: (h, 0, 0)),
        ],
        out_specs=pl.BlockSpec((tc, RET_DV), lambda b, h, t: (row(b, h, t), h)),
        scratch_shapes=[pltpu.VMEM((RET_DK, RET_DV), F32)],
        compiler_params=_params(("parallel", "parallel", "arbitrary")),
        name="retention",
    )(y, y, y, y, cos2, sin2, dmat, xi, zeta, cdec, nw.reshape(H, 1, RET_DV))


def _gelu_tanh(x):
    return 0.5 * x * (1.0 + jnp.tanh(0.7978845608028654 * (x + 0.044715 * x * x * x)))


def _nsa_compress_kernel(xk_ref, xv_ref, pek_ref, pev_ref, w1k_ref, w1v_ref, w2k_ref, w2v_ref,
                         ok_ref, ov_ref):
    gh = NSA_KV * NSA_CMP_HIDDEN

    def one(x_ref, pe_ref, w1_ref, w2_ref, o_ref):
        w1 = w1_ref[...]
        pq = jnp.dot(x_ref[...].astype(BF16), w1, preferred_element_type=F32)
        pe = jnp.dot(pe_ref[...].astype(BF16), w1, preferred_element_type=F32)
        const = pe[0:1, 0:gh] + pe[1:2, gh:]
        nrow = pq.shape[0]
        nxt = pltpu.roll(pq[:, gh:], nrow - 1, 0)
        hid = _gelu_tanh(pq[:, 0:gh] + nxt + const)
        o_ref[...] = jnp.dot(hid.astype(BF16), w2_ref[...], preferred_element_type=F32)

    one(xk_ref, pek_ref, w1k_ref, w2k_ref, ok_ref)
    one(xv_ref, pev_ref, w1v_ref, w2v_ref, ov_ref)


def _compress_weights(pe, w1, w2):
    G, L, S = NSA_KV, NSA_CMP_LEN, NSA_CMP_STRIDE
    d, hd = NSA_DH, NSA_CMP_HIDDEN
    eye = jnp.eye(G, dtype=F32)

    def big(w):
        return jnp.einsum('ldh,ge->lgdeh', w, eye).reshape(S * G * d, G * hd)

    w1big = jnp.concatenate([big(w1[:S]), big(w1[S:])], axis=1).astype(BF16)
    w2big = jnp.einsum('hd,ge->ghed', w2, eye).reshape(G * hd, G * d).astype(BF16)

    def pe_row(p):
        return jnp.broadcast_to(p[:, None, :], (S, G, d)).reshape(S * G * d)

    pe8 = jnp.zeros((8, S * G * d), F32).at[0].set(pe_row(pe[:S])).at[1].set(pe_row(pe[S:]))
    return pe8, w1big, w2big


def nsa_compress(y3, pe_k, w1_k, w2_k, pe_v, w1_v, w2_v):
    batch, seq, _ = y3.shape
    G, S, d = NSA_KV, NSA_CMP_STRIDE, NSA_DH
    nrow = seq // S
    xk = y3[:, :, 1024:1280].reshape(batch, nrow, S * G * d)
    xv = y3[:, :, 1280:1536].reshape(batch, nrow, S * G * d)
    pek, w1k, w2k = _compress_weights(pe_k, w1_k, w2_k)
    pev, w1v, w2v = _compress_weights(pe_v, w1_v, w2_v)
    kdim = S * G * d
    full = lambda shape: pl.BlockSpec(shape, lambda b: (0,) * len(shape))
    out = jax.ShapeDtypeStruct((batch, nrow, G * d), F32)
    return pl.pallas_call(
        _nsa_compress_kernel,
        out_shape=(out, out),
        grid=(batch,),
        in_specs=[
            pl.BlockSpec((None, nrow, kdim), lambda b: (b, 0, 0)),
            pl.BlockSpec((None, nrow, kdim), lambda b: (b, 0, 0)),
            full((8, kdim)), full((8, kdim)),
            full(w1k.shape), full(w1v.shape), full(w2k.shape), full(w2v.shape),
        ],
        out_specs=(pl.BlockSpec((None, nrow, G * d), lambda b: (b, 0, 0)),
                   pl.BlockSpec((None, nrow, G * d), lambda b: (b, 0, 0))),
        compiler_params=_params(("parallel",)),
        name="nsa_compress",
    )(xk, xv, pek, pev, w1k, w1v, w2k, w2v)


NSA_TQ = 128
NSA_TK = 256
SEL_FORCED = 1e30
SEL_FUTURE = -1e30
SEL_TAKEN = -3e38
SEL_MASK = -1e9


def _nsa_attn_kernel(q_ref, gt_ref, kc_ref, vc_ref, ks_ref, vs_ref, kw_ref, vw_ref, ov_ref,
                     o_ref, m_ref, l_ref, acc_ref, *, seq):
    TQ, TK, d = NSA_TQ, NSA_TK, NSA_DH
    R = NSA_HEADS // NSA_KV
    g = pl.program_id(1)
    i = pl.program_id(2)
    t0 = i * TQ
    scale = d ** -0.5
    q = q_ref[...]
    qs = jnp.concatenate([q[:, r * d:(r + 1) * d] for r in range(R)], axis=0).astype(BF16)

    ncp = kc_ref.shape[0]
    s1 = lax.dot_general(qs, kc_ref[...].astype(BF16), NT_DIMS, preferred_element_type=F32) * scale
    s1 = s1.reshape(R, TQ, ncp)
    tq1 = t0 + lax.broadcasted_iota(jnp.int32, (1, TQ, ncp), 1)
    nid = lax.broadcasted_iota(jnp.int32, (1, TQ, ncp), 2)
    s1 = jnp.where(nid * NSA_CMP_STRIDE + (NSA_CMP_LEN - 1) <= tq1, s1, -jnp.inf)
    m1 = jnp.max(s1, axis=-1, keepdims=True)
    m1 = jnp.where(m1 > -jnp.inf, m1, 0.0)
    e1 = jnp.exp(s1 - m1)
    z1 = jnp.sum(e1, axis=-1, keepdims=True)
    p1 = e1 / jnp.where(z1 > 0, z1, 1.0)
    o_cmp = jnp.dot(p1.reshape(R * TQ, ncp).astype(BF16), vc_ref[...].astype(BF16),
                    preferred_element_type=F32)

    imp = jnp.dot(jnp.sum(p1, axis=0), ov_ref[...], preferred_element_type=F32,
                  precision=lax.Precision.HIGHEST)
    nblk = seq // NSA_SEL_LEN
    blk = lax.broadcasted_iota(jnp.int32, (TQ, LANES), 1)
    cur = (t0 + lax.broadcasted_iota(jnp.int32, (TQ, LANES), 0)) // NSA_SEL_LEN
    imp = jnp.where(blk > cur, SEL_FUTURE, imp)
    imp = jnp.where((blk == 0) | (blk == cur) | (blk == cur - 1), SEL_FORCED, imp)
    imp = jnp.where(blk >= nblk, SEL_TAKEN, imp)
    sel = jnp.zeros((TQ, LANES), F32)
    for _ in range(min(NSA_N_SEL, nblk)):
        mx = jnp.max(imp, axis=-1, keepdims=True)
        first = jnp.min(jnp.where(imp == mx, blk, LANES), axis=-1, keepdims=True)
        hit = blk == first
        sel = jnp.where(hit, 1.0, sel)
        imp = jnp.where(hit, SEL_TAKEN, imp)
    nsl = (seq // NSA_SEL_LEN + LANES - 1) // LANES * LANES
    bias = jnp.where(sel > 0, 0.0, SEL_MASK)[:, 0:nsl].astype(BF16)
    q_aug = jnp.concatenate([qs, jnp.concatenate([bias] * R, axis=0)], axis=1)

    m_ref[...] = jnp.full_like(m_ref, -jnp.inf)
    l_ref[...] = jnp.zeros_like(l_ref)
    acc_ref[...] = jnp.zeros_like(acc_ref)
    trow = t0 + lax.broadcasted_iota(jnp.int32, (1, TQ, TK), 1)
    kcol = lax.broadcasted_iota(jnp.int32, (1, TQ, TK), 2)
    krow = lax.broadcasted_iota(jnp.int32, (TK, nsl), 0)
    bcol = lax.broadcasted_iota(jnp.int32, (TK, nsl), 1)

    def sel_step(j, carry):
        k0 = pl.multiple_of(j * TK, TK)
        onehot = jnp.where((k0 + krow) // NSA_SEL_LEN == bcol, 1.0, 0.0).astype(BF16)
        k_aug = jnp.concatenate([ks_ref[pl.ds(k0, TK), :].astype(BF16), onehot], axis=1)
        s = lax.dot_general(q_aug, k_aug, NT_DIMS, preferred_element_type=F32) * scale
        s = jnp.where(k0 + kcol <= trow, s.reshape(R, TQ, TK), -jnp.inf).reshape(R * TQ, TK)
        m_old = m_ref[...]
        m_new = jnp.maximum(m_old, jnp.max(s, axis=-1, keepdims=True))
        alpha = jnp.exp(m_old - m_new)
        p = jnp.exp(s - m_new)
        l_ref[...] = alpha * l_ref[...] + jnp.sum(p, axis=-1, keepdims=True)
        acc_ref[...] = alpha * acc_ref[...] + jnp.dot(
            p.astype(BF16), vs_ref[pl.ds(k0, TK), :].astype(BF16), preferred_element_type=F32)
        m_ref[...] = m_new
        return carry

    lax.fori_loop(0, (t0 + TQ + TK - 1) // TK, sel_step, 0)
    o_slc = acc_ref[...] / l_ref[...]

    W = NSA_WINDOW
    span = min(W + TQ, seq)
    kstart = pl.multiple_of(jnp.maximum(t0 + TQ - span, 0), TQ)
    s3 = lax.dot_general(qs, kw_ref[pl.ds(kstart, span), :].astype(BF16), NT_DIMS,
                         preferred_element_type=F32) * scale
    rel = (t0 + lax.broadcasted_iota(jnp.int32, (1, TQ, span), 1)
           - kstart - lax.broadcasted_iota(jnp.int32, (1, TQ, span), 2))
    s3 = jnp.where((rel >= 0) & (rel < W), s3.reshape(R, TQ, span), -jnp.inf)
    m3 = jnp.max(s3, axis=-1, keepdims=True)
    e3 = jnp.exp(s3 - m3)
    z3 = jnp.sum(e3, axis=-1, keepdims=True)
    o_win = jnp.dot(e3.reshape(R * TQ, span).astype(BF16), vw_ref[pl.ds(kstart, span), :].astype(BF16),
                    preferred_element_type=F32) / z3.reshape(R * TQ, 1)

    gt = _sigmoid(gt_ref[...])
    gsel = jnp.where(g == 0, gt[:, 0:3 * R], gt[:, 3 * R:6 * R])
    for r in range(R):
        rows = slice(r * TQ, (r + 1) * TQ)
        o = (gsel[:, 3 * r:3 * r + 1] * o_cmp[rows] + gsel[:, 3 * r + 1:3 * r + 2] * o_slc[rows]
             + gsel[:, 3 * r + 2:3 * r + 3] * o_win[rows])
        o_ref[:, r * d:(r + 1) * d] = o.astype(o_ref.dtype)


def nsa_attention(y3, k_cmp, v_cmp):
    batch, seq, n = y3.shape
    m = batch * seq
    TQ, d = NSA_TQ, NSA_DH
    R = NSA_HEADS // NSA_KV
    nq = seq // TQ
    ncp = k_cmp.shape[1]
    y2 = y3.reshape(m, n)
    nc = (seq - NSA_CMP_LEN) // NSA_CMP_STRIDE + 1
    cs = jnp.arange(ncp) * NSA_CMP_STRIDE
    ss = jnp.arange(LANES) * NSA_SEL_LEN
    overlap = ((cs[:, None] < ss[None, :] + NSA_SEL_LEN) & (cs[:, None] + NSA_CMP_LEN > ss[None, :])
               & (jnp.arange(ncp)[:, None] < nc) & (jnp.arange(LANES)[None, :] < seq // NSA_SEL_LEN))
    overlap = overlap.astype(F32)
    nsl = (seq // NSA_SEL_LEN + LANES - 1) // LANES * LANES
    kv_spec = lambda col: pl.BlockSpec((None, seq, d), lambda b, g, i: (b, 0, col + g))
    return pl.pallas_call(
        functools.partial(_nsa_attn_kernel, seq=seq),
        out_shape=jax.ShapeDtypeStruct((m, MIX_HALF), BF16),
        grid=(batch, NSA_KV, nq),
        in_specs=[
            pl.BlockSpec((TQ, R * d), lambda b, g, i: (b * nq + i, g)),
            pl.BlockSpec((TQ, LANES), lambda b, g, i: (b * nq + i, CD_GATE_COL // LANES)),
            pl.BlockSpec((None, ncp, d), lambda b, g, i: (b, 0, g)),
            pl.BlockSpec((None, ncp, d), lambda b, g, i: (b, 0, g)),
            kv_spec(1536 // d), kv_spec(1792 // d), kv_spec(2048 // d), kv_spec(2304 // d),
            pl.BlockSpec((ncp, LANES), lambda b, g, i: (0, 0)),
        ],
        out_specs=pl.BlockSpec((TQ, R * d), lambda b, g, i: (b * nq + i, g)),
        scratch_shapes=[pltpu.VMEM((R * TQ, 1), F32), pltpu.VMEM((R * TQ, 1), F32),
                        pltpu.VMEM((R * TQ, d), F32)],
        compiler_params=_params(("parallel", "parallel", "arbitrary")),
        name="nsa_attention",
    )(y2, y2, k_cmp, v_cmp, y3, y3, y3, y3, overlap)


def _router_kernel(x_ref, nw_ref, w2_ref, wh_ref, o_ref):
    x = x_ref[...]
    xn = x * lax.rsqrt(jnp.mean(x * x, axis=-1, keepdims=True) + RMS_EPS) * nw_ref[...]
    xh = xn.astype(BF16)
    xl = (xn - xh.astype(F32)).astype(BF16)
    r1 = jnp.dot(xh, w2_ref[...], preferred_element_type=F32)
    r2 = jnp.dot(xl, wh_ref[...], preferred_element_type=F32)
    o_ref[...] = r1[:, 0:LANES] + r1[:, LANES:] + r2


def router_logits(h, nw, w_grp, w_exp, tm):
    m, d = h.shape
    wr = jnp.zeros((d, LANES), F32).at[:, 0:MOE_GROUPS].set(w_grp)
    wr = wr.at[:, MOE_GROUPS:MOE_GROUPS + MOE_EXPERTS].set(w_exp)
    wh = wr.astype(BF16)
    wl = (wr - wh.astype(F32)).astype(BF16)
    return pl.pallas_call(
        _router_kernel,
        out_shape=jax.ShapeDtypeStruct((m, LANES), F32),
        grid=(m // tm,),
        in_specs=[
            pl.BlockSpec((tm, d), lambda i: (i, 0)),
            pl.BlockSpec((1, d), lambda i: (0, 0)),
            pl.BlockSpec((d, 2 * LANES), lambda i: (0, 0)),
            pl.BlockSpec((d, LANES), lambda i: (0, 0)),
        ],
        out_specs=pl.BlockSpec((tm, LANES), lambda i: (i, 0)),
        compiler_params=_params(("parallel",)),
        name="router_logits",
    )(h, nw.reshape(1, d), jnp.concatenate([wh, wl], axis=1), wh)


def _gather_rows(idx_ref, base, src_ref, dst_ref, sem, n_rows):
    def body(r, carry):
        tok = idx_ref[base + r]
        pltpu.make_async_copy(src_ref.at[pl.ds(tok, 1)], dst_ref.at[pl.ds(r, 1)], sem).start()
        return carry
    lax.fori_loop(0, n_rows, body, 0, unroll=8)


def _wait_rows(src_ref, dst_ref, sem, n_rows):
    pltpu.make_async_copy(src_ref.at[pl.ds(0, n_rows)], dst_ref, sem).wait()


def _expert_kernel(be_ref, nused_ref, tok_ref, h_ref, nw_ref, wg_ref, wu_ref, wd_ref, o_ref,
                   xbuf, sems):
    TM = MOE_TM
    i = pl.program_id(0)
    n_used = nused_ref[0]
    slot = i % 2

    @pl.when(i == 0)
    def _():
        _gather_rows(tok_ref, 0, h_ref, xbuf.at[0], sems.at[0], TM)

    @pl.when(i + 1 < n_used)
    def _():
        _gather_rows(tok_ref, (i + 1) * TM, h_ref, xbuf.at[1 - slot], sems.at[1 - slot], TM)

    @pl.when(i < n_used)
    def _():
        _wait_rows(h_ref, xbuf.at[slot], sems.at[slot], TM)
        x = xbuf[slot]
        xn = (x * lax.rsqrt(jnp.mean(x * x, axis=-1, keepdims=True) + RMS_EPS) * nw_ref[...]).astype(BF16)
        hg = jnp.dot(xn, wg_ref[...], preferred_element_type=F32)
        hu = jnp.dot(xn, wu_ref[...], preferred_element_type=F32)
        hb = (_silu(hg) * hu).astype(BF16)
        o_ref[...] = jnp.dot(hb, wd_ref[...], preferred_element_type=F32)

    @pl.when(i >= n_used)
    def _():
        o_ref[...] = jnp.zeros_like(o_ref)


def expert_ffn(h, nw, row_tok, blk_expert, n_used, w_gate, w_up, w_down):
    m, d = h.shape
    rows = row_tok.shape[0]
    TM = MOE_TM
    nb = rows // TM
    dff = w_gate.shape[2]
    grid_spec = pltpu.PrefetchScalarGridSpec(
        num_scalar_prefetch=3,
        grid=(nb,),
        in_specs=[
            pl.BlockSpec(memory_space=pl.ANY),
            pl.BlockSpec((1, d), lambda i, be, nu, tok: (0, 0)),
            pl.BlockSpec((None, d, dff), lambda i, be, nu, tok: (be[i], 0, 0)),
            pl.BlockSpec((None, d, dff), lambda i, be, nu, tok: (be[i], 0, 0)),
            pl.BlockSpec((None, dff, d), lambda i, be, nu, tok: (be[i], 0, 0)),
        ],
        out_specs=pl.BlockSpec((TM, d), lambda i, be, nu, tok: (i, 0)),
        scratch_shapes=[pltpu.VMEM((2, TM, d), F32), pltpu.SemaphoreType.DMA((2,))],
    )
    return pl.pallas_call(
        _expert_kernel,
        out_shape=jax.ShapeDtypeStruct((rows, d), F32),
        grid_spec=grid_spec,
        compiler_params=_params(("arbitrary",)),
        name="expert_ffn",
    )(blk_expert, n_used, row_tok, h, nw.reshape(1, d), w_gate, w_up, w_down)


def _combine_kernel(dest_ref, ys_ref, h_ref, gate_ref, fw_ref, o_ref, ybuf, sems, *, final_norm):
    tc = h_ref.shape[0]
    i = pl.program_id(0)
    n = pl.num_programs(0)
    slot = i % 2

    @pl.when(i == 0)
    def _():
        _gather_rows(dest_ref, 0, ys_ref, ybuf.at[0], sems.at[0], 2 * tc)

    @pl.when(i + 1 < n)
    def _():
        _gather_rows(dest_ref, (i + 1) * 2 * tc, ys_ref, ybuf.at[1 - slot], sems.at[1 - slot], 2 * tc)

    _wait_rows(ys_ref, ybuf.at[slot], sems.at[slot], 2 * tc)
    gate = gate_ref[...]
    out = h_ref[...] + (ybuf[slot, 0:tc, :] * gate[:, 0:1] + ybuf[slot, tc:, :] * gate[:, 1:2])
    if final_norm:
        out = out * lax.rsqrt(jnp.mean(out * out, axis=-1, keepdims=True) + RMS_EPS) * fw_ref[...]
    o_ref[...] = out


def moe_combine(h, ys, dest_tiles, gate, final_w, tc, final_norm):
    m, d = h.shape
    grid_spec = pltpu.PrefetchScalarGridSpec(
        num_scalar_prefetch=1,
        grid=(m // tc,),
        in_specs=[
            pl.BlockSpec(memory_space=pl.ANY),
            pl.BlockSpec((tc, d), lambda i, dst: (i, 0)),
            pl.BlockSpec((tc, MOE_TOPK), lambda i, dst: (i, 0)),
            pl.BlockSpec((1, d), lambda i, dst: (0, 0)),
        ],
        out_specs=pl.BlockSpec((tc, d), lambda i, dst: (i, 0)),
        scratch_shapes=[pltpu.VMEM((2, 2 * tc, d), F32), pltpu.SemaphoreType.DMA((2,))],
    )
    return pl.pallas_call(
        functools.partial(_combine_kernel, final_norm=final_norm),
        out_shape=jax.ShapeDtypeStruct((m, d), F32),
        grid_spec=grid_spec,
        compiler_params=_params(("arbitrary",)),
        name="moe_combine",
    )(dest_tiles, ys, h, gate, final_w.reshape(1, d))


def _route(logits, b_grp, b_exp):
    n = logits.shape[0]
    K, E, TM = MOE_TOPK, MOE_EXPERTS, MOE_TM
    grp_logits = logits[:, 0:MOE_GROUPS] + b_grp.astype(F32)
    grp_prob = jax.nn.softmax(grp_logits, axis=-1)
    g_idx = jnp.argmax(grp_logits, axis=-1)
    p_grp = jnp.take_along_axis(grp_prob, g_idx[:, None], axis=1)[:, 0]
    exp_logits = (logits[:, MOE_GROUPS:MOE_GROUPS + E] + b_exp.astype(F32)).reshape(n, MOE_GROUPS, MOE_EPG)
    within = jnp.take_along_axis(exp_logits, g_idx[:, None, None], axis=1)[:, 0]
    top_p, top_i = lax.top_k(jax.nn.softmax(within, axis=-1), K)
    gate = p_grp[:, None] * top_p / jnp.sum(top_p, axis=-1, keepdims=True)
    expert = (g_idx[:, None] * MOE_EPG + top_i).astype(jnp.int32)

    A = n * K
    flat_e = expert.reshape(A)
    order = jnp.argsort(flat_e)
    e_sorted = flat_e[order]
    counts = jnp.bincount(flat_e, length=E)
    padded = (counts + TM - 1) // TM * TM
    pad_end = jnp.cumsum(padded)
    pad_start = pad_end - padded
    start = jnp.cumsum(counts) - counts
    dest_sorted = (pad_start[e_sorted] + jnp.arange(A) - start[e_sorted]).astype(jnp.int32)
    rows = A + E * TM
    nb = rows // TM
    row_tok = jnp.zeros((rows,), jnp.int32).at[dest_sorted].set((order // K).astype(jnp.int32))
    dest = jnp.zeros((A,), jnp.int32).at[order].set(dest_sorted)
    blk_expert = jnp.minimum(jnp.searchsorted(pad_end, jnp.arange(nb) * TM, side='right'), E - 1)
    n_used = (pad_end[-1] // TM).astype(jnp.int32).reshape(1)
    return gate.astype(F32), dest.reshape(n, K), row_tok, blk_expert.astype(jnp.int32), n_used


def hier_moe(h, nw, w_grp, b_grp, w_exp, b_exp, w_gate, w_up, w_down, final_w, final_norm):
    m, d = h.shape
    tc = 256
    logits = router_logits(h, nw, w_grp, w_exp, 512)
    gate, dest, row_tok, blk_expert, n_used = _route(logits, b_grp, b_exp)
    ys = expert_ffn(h, nw, row_tok, blk_expert, n_used,
                    w_gate.astype(BF16), w_up.astype(BF16), w_down.astype(BF16))
    dest_tiles = dest.reshape(m // tc, tc, MOE_TOPK).transpose(0, 2, 1).reshape(-1)
    return moe_combine(h, ys, dest_tiles, gate, final_w, tc, final_norm)


def _cd_in_weights(w):
    gates = w[:, 2560:2584]
    pad = jnp.zeros((w.shape[0], CD_N - CD_GATE_COL - gates.shape[1]), w.dtype)
    return jnp.concatenate([w[:, 0:2560], w[:, 2584:], gates, pad], axis=1)


def mixer_ab(h, nw, w_in, w_out, lb, hg_norm, sw_sinks, batch, seq):
    y = norm_matmul(h, nw, w_in.astype(BF16), 512, 1408)
    o_a = hgrn2(y, lb, hg_norm, batch, seq, min(512, seq))
    o_b = swa(y, sw_sinks, batch, seq)
    return outproj_residual(o_a, o_b, w_out.astype(BF16), h, 512)


def mixer_cd(h, nw, w_in, w_out, pe_k, w1_k, w2_k, pe_v, w1_v, w2_v, ret_norm, batch, seq):
    y = norm_matmul(h, nw, _cd_in_weights(w_in).astype(BF16), 512, 1920)
    y3 = y.reshape(batch, seq, CD_N)
    k_cmp, v_cmp = nsa_compress(y3, pe_k, w1_k, w2_k, pe_v, w1_v, w2_v)
    o_c = nsa_attention(y3, k_cmp, v_cmp)
    o_d = retention(y, ret_norm, batch, seq, min(512, seq))
    return outproj_residual(o_c, o_d, w_out.astype(BF16), h, 512)


def kernel(x, norm_mix, norm_ffn, norm_final, ab_w_in, ab_w_out, hg_lb_logits, hg_norm, sw_sinks, cd_w_in, cd_w_out, nsa_pe_k, nsa_w1_k, nsa_w2_k, nsa_pe_v, nsa_w1_v, nsa_w2_v, ret_norm, moe_w_grp, moe_b_grp, moe_w_exp, moe_b_exp, moe_w_gate, moe_w_up, moe_w_down):
    batch, seq, d = x.shape
    depth = norm_mix.shape[0]
    lb_all = jnp.cumsum(jax.nn.softmax(hg_lb_logits.astype(F32), axis=0), axis=0)
    h = x.reshape(batch * seq, d)
    for layer in range(depth):
        if layer % 2 == 0:
            e = layer // 2
            h = mixer_ab(h, norm_mix[layer], ab_w_in[e], ab_w_out[e], lb_all[layer], hg_norm[e],
                         sw_sinks[e], batch, seq)
        else:
            o = layer // 2
            h = mixer_cd(h, norm_mix[layer], cd_w_in[o], cd_w_out[o], nsa_pe_k[o], nsa_w1_k[o],
                         nsa_w2_k[o], nsa_pe_v[o], nsa_w1_v[o], nsa_w2_v[o], ret_norm[o], batch, seq)
        h = hier_moe(h, norm_ffn[layer], moe_w_grp[layer], moe_b_grp[layer], moe_w_exp[layer],
                     moe_b_exp[layer], moe_w_gate[layer], moe_w_up[layer], moe_w_down[layer],
                     norm_final, layer == depth - 1)
    return h.reshape(batch, seq, d)
```

```python
import functools

import jax
import jax.numpy as jnp
from jax import lax
from jax.experimental import pallas as pl
from jax.experimental.pallas import tpu as pltpu

F32 = jnp.float32
BF16 = jnp.bfloat16

D_MODEL = 2048
MIX_HALF = D_MODEL // 2
RMS_EPS = 1e-6
LANES = 128
VMEM_LIMIT = 56 * 1024 * 1024

HG_DK = 128
HG_HEADS = 8
HG_CHUNK = 64
HG_SUB = 16
SW_DH = 128
SW_HEADS = 8
SW_KV = 2
SW_WINDOW = 128
NSA_DH = 128
NSA_HEADS = 8
NSA_KV = 2
NSA_CMP_LEN = 32
NSA_CMP_STRIDE = 16
NSA_CMP_HIDDEN = 128
NSA_SEL_LEN = 64
NSA_N_SEL = 8
NSA_WINDOW = 512
RET_DK = 128
RET_DV = 256
RET_HEADS = 4
RET_CHUNK = 128
RET_THETA_BASE = 10000.0
MOE_GROUPS = 4
MOE_EPG = 8
MOE_EXPERTS = 32
MOE_TOPK = 2
MOE_DFF = 1024
MOE_TM = 256

AB_N = 5632
CD_N = 5760
CD_GATE_COL = 5632

NT_DIMS = (((1,), (1,)), ((), ()))
TN_DIMS = (((0,), (0,)), ((), ()))


def _params(sem, vmem=VMEM_LIMIT):
    return pltpu.CompilerParams(dimension_semantics=sem, vmem_limit_bytes=vmem)


def _sigmoid(x):
    return 1.0 / (1.0 + jnp.exp(-x))


def _silu(x):
    return x * _sigmoid(x)


def _dot_exact01(a01, x):
    n = x.shape[1]
    hi = x.astype(BF16)
    rem = x - hi.astype(F32)
    mid = rem.astype(BF16)
    lo = (rem - mid.astype(F32)).astype(BF16)
    r = jnp.dot(a01, jnp.concatenate([hi, mid, lo], axis=1), preferred_element_type=F32)
    return r[:, 0:n] + r[:, n:2 * n] + r[:, 2 * n:]


def _norm_matmul_kernel(x_ref, nw_ref, w_ref, o_ref, xn_ref):
    @pl.when(pl.program_id(1) == 0)
    def _():
        x = x_ref[...]
        ms = jnp.mean(x * x, axis=-1, keepdims=True)
        xn_ref[...] = (x * lax.rsqrt(ms + RMS_EPS) * nw_ref[...]).astype(BF16)

    o_ref[...] = jnp.dot(xn_ref[...], w_ref[...], preferred_element_type=F32)


def norm_matmul(x, nw, w, tm, tn):
    m, d = x.shape
    n = w.shape[1]
    return pl.pallas_call(
        _norm_matmul_kernel,
        out_shape=jax.ShapeDtypeStruct((m, n), F32),
        grid=(m // tm, n // tn),
        in_specs=[
            pl.BlockSpec((tm, d), lambda i, j: (i, 0)),
            pl.BlockSpec((1, d), lambda i, j: (0, 0)),
            pl.BlockSpec((d, tn), lambda i, j: (0, j)),
        ],
        out_specs=pl.BlockSpec((tm, tn), lambda i, j: (i, j)),
        scratch_shapes=[pltpu.VMEM((tm, d), BF16)],
        compiler_params=_params(("parallel", "arbitrary")),
        name="norm_matmul",
    )(x, nw.reshape(1, d), w)


def _outproj_kernel(a_ref, b_ref, w_ref, h_ref, o_ref):
    half = a_ref.shape[1]
    acc = jnp.dot(a_ref[...], w_ref[0:half, :], preferred_element_type=F32)
    acc = acc + jnp.dot(b_ref[...], w_ref[half:, :], preferred_element_type=F32)
    o_ref[...] = h_ref[...] + acc


def outproj_residual(a, b, w, h, tm):
    m, half = a.shape
    d = w.shape[1]
    return pl.pallas_call(
        _outproj_kernel,
        out_shape=jax.ShapeDtypeStruct((m, d), F32),
        grid=(m // tm,),
        in_specs=[
            pl.BlockSpec((tm, half), lambda i: (i, 0)),
            pl.BlockSpec((tm, half), lambda i: (i, 0)),
            pl.BlockSpec((2 * half, d), lambda i: (0, 0)),
            pl.BlockSpec((tm, d), lambda i: (i, 0)),
        ],
        out_specs=pl.BlockSpec((tm, d), lambda i: (i, 0)),
        compiler_params=_params(("parallel",)),
        name="outproj_residual",
    )(a, b, w, h)


def _hgrn2_kernel(q_ref, f_ref, i_ref, g_ref, lb_ref, nw_ref, o_ref, st_ref, *, n_chunks):
    C, c = HG_CHUNK, HG_SUB
    nsub = C // c

    @pl.when(pl.program_id(2) == 0)
    def _():
        st_ref[...] = jnp.zeros_like(st_ref)

    lb = lb_ref[...]
    nw = nw_ref[...]
    tri = (lax.broadcasted_iota(jnp.int32, (C, C), 0)
           >= lax.broadcasted_iota(jnp.int32, (C, C), 1)).astype(BF16)
    hs = c // 2
    row8 = lax.broadcasted_iota(jnp.int32, (1, hs, 1), 1)
    chunks = range(n_chunks)
    rows_of = lambda ci: slice(ci * C, (ci + 1) * C)

    q = [q_ref[rows_of(ci), :] for ci in chunks]
    v = [i_ref[rows_of(ci), :] for ci in chunks]
    vb = [x.astype(BF16) for x in v]
    f = [lb + (1.0 - lb) * _sigmoid(f_ref[rows_of(ci), :]) for ci in chunks]
    k = [1.0 - x for x in f]
    b = [_dot_exact01(tri, jnp.log(x)) for x in f]

    a_off = []
    for ci in chunks:
        for i in range(1, nsub):
            r = b[ci][i * c - 1:i * c, :]
            qi = q[ci][i * c:(i + 1) * c] * jnp.exp(b[ci][i * c:(i + 1) * c] - r)
            ki = k[ci][0:i * c] * jnp.exp(r - b[ci][0:i * c])
            a_off.append(lax.dot_general(qi.astype(BF16), ki.astype(BF16), NT_DIMS,
                                         preferred_element_type=F32))
    o_off = []
    for ci in chunks:
        rows = [jnp.zeros((c, HG_DK), F32)]
        for i in range(1, nsub):
            a = a_off[ci * (nsub - 1) + i - 1]
            rows.append(jnp.dot(a.astype(BF16), vb[ci][0:i * c], preferred_element_type=F32))
        o_off.append(jnp.concatenate(rows, axis=0))
    inc = []
    for ci in chunks:
        bend = b[ci][C - 1:C, :]
        kd = k[ci] * jnp.exp(bend - b[ci])
        inc.append(lax.dot_general(vb[ci], kd.astype(BF16), TN_DIMS, preferred_element_type=F32))
    st = st_ref[...]
    o_st = []
    for ci in chunks:
        o_st.append(lax.dot_general((q[ci] * jnp.exp(b[ci])).astype(BF16), st.astype(BF16), NT_DIMS,
                                    preferred_element_type=F32))
        st = st * jnp.exp(b[ci][C - 1:C, :]) + inc[ci]
    st_ref[...] = st

    for ci in chunks:
        b4 = b[ci].reshape(nsub, c, HG_DK)
        q4 = q[ci].reshape(nsub, c, HG_DK)
        k4 = k[ci].reshape(nsub, c, HG_DK)
        v4 = v[ci].reshape(nsub, c, HG_DK)
        bt, bb = b4[:, 0:hs], b4[:, hs:]
        qt, qb = q4[:, 0:hs], q4[:, hs:]
        od_t = jnp.zeros((nsub, hs, HG_DK), F32)
        od_b = jnp.zeros((nsub, hs, HG_DK), F32)
        for s in range(c):
            piv, kp, vp = b4[:, s:s + 1], k4[:, s:s + 1], v4[:, s:s + 1]
            if s < hs:
                e = jnp.exp(jnp.minimum(bt - piv, 0.0))
                a = jnp.sum(qt * e * kp, axis=-1, keepdims=True)
                od_t = od_t + jnp.where(row8 >= s, a, 0.0) * vp
                e = jnp.exp(jnp.minimum(bb - piv, 0.0))
                od_b = od_b + jnp.sum(qb * e * kp, axis=-1, keepdims=True) * vp
            else:
                e = jnp.exp(jnp.minimum(bb - piv, 0.0))
                a = jnp.sum(qb * e * kp, axis=-1, keepdims=True)
                od_b = od_b + jnp.where(row8 >= s - hs, a, 0.0) * vp
        o = o_off[ci] + o_st[ci] + jnp.concatenate([od_t, od_b], axis=1).reshape(C, HG_DK)
        y = o * lax.rsqrt(jnp.mean(o * o, axis=-1, keepdims=True) + RMS_EPS) * nw
        o_ref[rows_of(ci), :] = (y * _silu(g_ref[rows_of(ci), :])).astype(o_ref.dtype)


def hgrn2(y, lb, nw, batch, seq, tc):
    m = y.shape[0]
    nt = seq // tc
    row = lambda b, h, t: b * nt + t
    hd = HG_HEADS
    return pl.pallas_call(
        functools.partial(_hgrn2_kernel, n_chunks=tc // HG_CHUNK),
        out_shape=jax.ShapeDtypeStruct((m, MIX_HALF), BF16),
        grid=(batch, hd, nt),
        in_specs=[
            pl.BlockSpec((tc, HG_DK), lambda b, h, t: (row(b, h, t), h)),
            pl.BlockSpec((tc, HG_DK), lambda b, h, t: (row(b, h, t), hd + h)),
            pl.BlockSpec((tc, HG_DK), lambda b, h, t: (row(b, h, t), 2 * hd + h)),
            pl.BlockSpec((tc, HG_DK), lambda b, h, t: (row(b, h, t), 3 * hd + h)),
            pl.BlockSpec((None, 1, HG_DK), lambda b, h, t: (h, 0, 0)),
            pl.BlockSpec((None, 1, HG_DK), lambda b, h, t: (h, 0, 0)),
        ],
        out_specs=pl.BlockSpec((tc, HG_DK), lambda b, h, t: (row(b, h, t), h)),
        scratch_shapes=[pltpu.VMEM((HG_DK, HG_DK), F32)],
        compiler_params=_params(("parallel", "parallel", "arbitrary")),
        name="hgrn2",
    )(y, y, y, y, lb.reshape(hd, 1, HG_DK), nw.reshape(hd, 1, HG_DK))


def _swa_kernel(q_ref, kp_ref, kc_ref, vp_ref, vc_ref, sink_ref, o_ref):
    W = SW_WINDOW
    R = SW_HEADS // SW_KV
    n = pl.program_id(2)
    q = q_ref[...]
    qs = jnp.concatenate([q[:, r * SW_DH:(r + 1) * SW_DH] for r in range(R)], axis=0).astype(BF16)
    kk = jnp.concatenate([kp_ref[...], kc_ref[...]], axis=0).astype(BF16)
    vv = jnp.concatenate([vp_ref[...], vc_ref[...]], axis=0).astype(BF16)
    s = lax.dot_general(qs, kk, NT_DIMS, preferred_element_type=F32) * (SW_DH ** -0.5)
    s = s.reshape(R, W, 2 * W)
    tq = lax.broadcasted_iota(jnp.int32, (1, W, 2 * W), 1)
    kj = lax.broadcasted_iota(jnp.int32, (1, W, 2 * W), 2)
    rel = tq + W - kj
    valid = (rel >= 0) & (rel < W) & ((kj >= W) | (n > 0))
    s = jnp.where(valid, s, -jnp.inf)
    sink = sink_ref[...].reshape(R, W, 1)
    mx = jnp.maximum(jnp.max(s, axis=-1, keepdims=True), sink)
    p = jnp.exp(s - mx)
    z = jnp.sum(p, axis=-1, keepdims=True) + jnp.exp(sink - mx)
    o = jnp.dot(p.reshape(R * W, 2 * W).astype(BF16), vv, preferred_element_type=F32)
    o = o / z.reshape(R * W, 1)
    for r in range(R):
        o_ref[:, r * SW_DH:(r + 1) * SW_DH] = o[r * W:(r + 1) * W].astype(o_ref.dtype)


def swa(y, sinks, batch, seq):
    m = y.shape[0]
    W = SW_WINDOW
    R = SW_HEADS // SW_KV
    nb = seq // W
    qcol = 4 * MIX_HALF // (R * SW_DH)
    kcol = (4 * MIX_HALF + SW_HEADS * SW_DH) // SW_DH
    vcol = kcol + SW_KV
    cur = lambda b, g, n: b * nb + n
    prev = lambda b, g, n: b * nb + jnp.maximum(n - 1, 0)
    sink_rows = jnp.repeat(sinks.astype(F32).reshape(SW_KV, R), W, axis=1).reshape(SW_KV, R * W, 1)
    return pl.pallas_call(
        _swa_kernel,
        out_shape=jax.ShapeDtypeStruct((m, MIX_HALF), BF16),
        grid=(batch, SW_KV, nb),
        in_specs=[
            pl.BlockSpec((W, R * SW_DH), lambda b, g, n: (cur(b, g, n), qcol + g)),
            pl.BlockSpec((W, SW_DH), lambda b, g, n: (prev(b, g, n), kcol + g)),
            pl.BlockSpec((W, SW_DH), lambda b, g, n: (cur(b, g, n), kcol + g)),
            pl.BlockSpec((W, SW_DH), lambda b, g, n: (prev(b, g, n), vcol + g)),
            pl.BlockSpec((W, SW_DH), lambda b, g, n: (cur(b, g, n), vcol + g)),
            pl.BlockSpec((None, R * W, 1), lambda b, g, n: (g, 0, 0)),
        ],
        out_specs=pl.BlockSpec((W, R * SW_DH), lambda b, g, n: (cur(b, g, n), g)),
        compiler_params=_params(("parallel", "parallel", "parallel")),
        name="swa",
    )(y, y, y, y, y, sink_rows)


def _retention_kernel(q_ref, k_ref, v_ref, g_ref, cos_ref, sin_ref, dmat_ref, xi_ref, zeta_ref,
                      cdec_ref, nw_ref, o_ref, st_ref, *, n_chunks):
    C = RET_CHUNK

    @pl.when(pl.program_id(2) == 0)
    def _():
        st_ref[...] = jnp.zeros_like(st_ref)

    dmat = dmat_ref[...]
    xi = xi_ref[...]
    zeta = zeta_ref[...]
    cdec = cdec_ref[...]
    nw = nw_ref[...]

    def chunk(ci, carry):
        r0 = pl.multiple_of(ci * C, C)
        cos2 = cos_ref[pl.ds(r0, C), :]
        sin2 = sin_ref[pl.ds(r0, C), :]
        q = q_ref[pl.ds(r0, C), :]
        k = k_ref[pl.ds(r0, C), :]
        qr = q * cos2 + pltpu.roll(q, RET_DK // 2, 1) * sin2
        kr = (k * cos2 + pltpu.roll(k, RET_DK // 2, 1) * sin2) * (RET_DK ** -0.5)
        vb = v_ref[pl.ds(r0, C), :].astype(BF16)
        st = st_ref[...]
        inner = lax.dot_general(qr.astype(BF16), kr.astype(BF16), NT_DIMS,
                                preferred_element_type=F32) * dmat
        o = jnp.dot(inner.astype(BF16), vb, preferred_element_type=F32)
        o = o + jnp.dot((qr * xi).astype(BF16), st.astype(BF16), preferred_element_type=F32)
        st_ref[...] = cdec * st + lax.dot_general((kr * zeta).astype(BF16), vb, TN_DIMS,
                                                  preferred_element_type=F32)
        cen = o - jnp.mean(o, axis=-1, keepdims=True)
        y = cen * lax.rsqrt(jnp.mean(cen * cen, axis=-1, keepdims=True) + RMS_EPS) * nw
        o_ref[pl.ds(r0, C), :] = (y * _silu(g_ref[pl.ds(r0, C), :])).astype(o_ref.dtype)
        return carry

    lax.fori_loop(0, n_chunks, chunk, 0)


def retention(y, nw, batch, seq, tc):
    m = y.shape[0]
    nt = seq // tc
    H, C = RET_HEADS, RET_CHUNK
    log_g = jnp.log1p(-jnp.exp2(-5.0 - jnp.arange(H, dtype=F32)))
    j = jnp.arange(C, dtype=F32)
    diff = j[:, None] - j[None, :]
    dmat = jnp.where(diff >= 0, jnp.exp(jnp.maximum(diff, 0.0)[None] * log_g[:, None, None]), 0.0)
    zeta = jnp.exp((C - 1 - j)[None, :] * log_g[:, None]).reshape(H, C, 1)
    xi = jnp.exp((j + 1.0)[None, :] * log_g[:, None]).reshape(H, C, 1)
    cdec = jnp.broadcast_to(jnp.exp(C * log_g)[:, None, None], (H, 1, RET_DV))
    half = RET_DK // 2
    theta = 1.0 / (RET_THETA_BASE ** jnp.linspace(0.0, 1.0, half, dtype=F32))
    ang = jnp.arange(seq).astype(F32)[:, None] * theta[None, :]
    cos2 = jnp.concatenate([jnp.cos(ang), jnp.cos(ang)], axis=-1)
    sin2 = jnp.concatenate([-jnp.sin(ang), jnp.sin(ang)], axis=-1)

    row = lambda b, h, t: b * nt + t
    qcol = 2560 // RET_DK
    kcol = 3072 // RET_DK
    vcol = 3584 // RET_DV
    gcol = 4608 // RET_DV
    return pl.pallas_call(
        functools.partial(_retention_kernel, n_chunks=tc // C),
        out_shape=jax.ShapeDtypeStruct((m, MIX_HALF), BF16),
        grid=(batch, H, nt),
        in_specs=[
            pl.BlockSpec((tc, RET_DK), lambda b, h, t: (row(b, h, t), qcol + h)),
            pl.BlockSpec((tc, RET_DK), lambda b, h, t: (row(b, h, t), kcol + h)),
            pl.BlockSpec((tc, RET_DV), lambda b, h, t: (row(b, h, t), vcol + h)),
            pl.BlockSpec((tc, RET_DV), lambda b, h, t: (row(b, h, t), gcol + h)),
            pl.BlockSpec((tc, RET_DK), lambda b, h, t: (t, 0)),
            pl.BlockSpec((tc, RET_DK), lambda b, h, t: (t, 0)),
            pl.BlockSpec((None, C, C), lambda b, h, t: (h, 0, 0)),
            pl.BlockSpec((None, C, 1), lambda b, h, t: (h, 0, 0)),
            pl.BlockSpec((None, C, 1), lambda b, h, t: (h, 0, 0)),
            pl.BlockSpec((None, 1, RET_DV), lambda b, h, t: (h, 0, 0)),
            pl.BlockSpec((None, 1, RET_DV), lambda b, h, t: (h, 0, 0)),
        ],
        out_specs=pl.BlockSpec((tc, RET_DV), lambda b, h, t: (row(b, h, t), h)),
        scratch_shapes=[pltpu.VMEM((RET_DK, RET_DV), F32)],
        compiler_params=_params(("parallel", "parallel", "arbitrary")),
        name="retention",
    )(y, y, y, y, cos2, sin2, dmat, xi, zeta, cdec, nw.reshape(H, 1, RET_DV))


def _gelu_tanh(x):
    return 0.5 * x * (1.0 + jnp.tanh(0.7978845608028654 * (x + 0.044715 * x * x * x)))


def _nsa_compress_kernel(xk_ref, xv_ref, pek_ref, pev_ref, w1k_ref, w1v_ref, w2k_ref, w2v_ref,
                         ok_ref, ov_ref):
    gh = NSA_KV * NSA_CMP_HIDDEN

    def one(x_ref, pe_ref, w1_ref, w2_ref, o_ref):
        w1 = w1_ref[...]
        pq = jnp.dot(x_ref[...].astype(BF16), w1, preferred_element_type=F32)
        pe = jnp.dot(pe_ref[...].astype(BF16), w1, preferred_element_type=F32)
        const = pe[0:1, 0:gh] + pe[1:2, gh:]
        nrow = pq.shape[0]
        nxt = pltpu.roll(pq[:, gh:], nrow - 1, 0)
        hid = _gelu_tanh(pq[:, 0:gh] + nxt + const)
        o_ref[...] = jnp.dot(hid.astype(BF16), w2_ref[...], preferred_element_type=F32)

    one(xk_ref, pek_ref, w1k_ref, w2k_ref, ok_ref)
    one(xv_ref, pev_ref, w1v_ref, w2v_ref, ov_ref)


def _compress_weights(pe, w1, w2):
    G, L, S = NSA_KV, NSA_CMP_LEN, NSA_CMP_STRIDE
    d, hd = NSA_DH, NSA_CMP_HIDDEN
    eye = jnp.eye(G, dtype=F32)

    def big(w):
        return jnp.einsum('ldh,ge->lgdeh', w, eye).reshape(S * G * d, G * hd)

    w1big = jnp.concatenate([big(w1[:S]), big(w1[S:])], axis=1).astype(BF16)
    w2big = jnp.einsum('hd,ge->ghed', w2, eye).reshape(G * hd, G * d).astype(BF16)

    def pe_row(p):
        return jnp.broadcast_to(p[:, None, :], (S, G, d)).reshape(S * G * d)

    pe8 = jnp.zeros((8, S * G * d), F32).at[0].set(pe_row(pe[:S])).at[1].set(pe_row(pe[S:]))
    return pe8, w1big, w2big


def nsa_compress(y3, pe_k, w1_k, w2_k, pe_v, w1_v, w2_v):
    batch, seq, _ = y3.shape
    G, S, d = NSA_KV, NSA_CMP_STRIDE, NSA_DH
    nrow = seq // S
    xk = y3[:, :, 1024:1280].reshape(batch, nrow, S * G * d)
    xv = y3[:, :, 1280:1536].reshape(batch, nrow, S * G * d)
    pek, w1k, w2k = _compress_weights(pe_k, w1_k, w2_k)
    pev, w1v, w2v = _compress_weights(pe_v, w1_v, w2_v)
    kdim = S * G * d
    full = lambda shape: pl.BlockSpec(shape, lambda b: (0,) * len(shape))
    out = jax.ShapeDtypeStruct((batch, nrow, G * d), F32)
    return pl.pallas_call(
        _nsa_compress_kernel,
        out_shape=(out, out),
        grid=(batch,),
        in_specs=[
            pl.BlockSpec((None, nrow, kdim), lambda b: (b, 0, 0)),
            pl.BlockSpec((None, nrow, kdim), lambda b: (b, 0, 0)),
            full((8, kdim)), full((8, kdim)),
            full(w1k.shape), full(w1v.shape), full(w2k.shape), full(w2v.shape),
        ],
        out_specs=(pl.BlockSpec((None, nrow, G * d), lambda b: (b, 0, 0)),
                   pl.BlockSpec((None, nrow, G * d), lambda b: (b, 0, 0))),
        compiler_params=_params(("parallel",)),
        name="nsa_compress",
    )(xk, xv, pek, pev, w1k, w1v, w2k, w2v)


NSA_TQ = 128
NSA_TK = 512
SEL_SHIFT = 6
SEL_FORCED = 1e30
SEL_FUTURE = -1e30
SEL_TAKEN = -3e38
SEL_MASK = -1e9
LOG2E = 1.4426950408889634


def _nsa_attn_kernel(q_ref, gt_ref, kc_ref, vct_ref, ks_ref, vs_ref, kw_ref, vw_ref, ovt_ref,
                     o_ref, kaug_ref, vst_ref, kwb_ref, vwt_ref, acc_ref, *, seq):
    TQ, TK, d = NSA_TQ, NSA_TK, NSA_DH
    R = NSA_HEADS // NSA_KV
    NQ = R * TQ
    g = pl.program_id(1)
    i = pl.program_id(2)
    t0 = i * TQ
    scale = d ** -0.5

    @pl.when(i == 0)
    def _():
        krow = lax.broadcasted_iota(jnp.int32, (TK, LANES), 0)
        bcol = lax.broadcasted_iota(jnp.int32, (TK, LANES), 1)

        def prep(c, carry):
            k0 = pl.multiple_of(c * TK, TK)
            onehot = jnp.where(((k0 + krow) >> SEL_SHIFT) == bcol, 1.0, 0.0).astype(BF16)
            kaug_ref[c] = jnp.concatenate([ks_ref[pl.ds(k0, TK), :].astype(BF16), onehot], axis=1)
            vst_ref[c] = vs_ref[pl.ds(k0, TK), :].T.astype(BF16)
            kwb_ref[pl.ds(k0, TK), :] = kw_ref[pl.ds(k0, TK), :].astype(BF16)
            for h in range(TK // TQ):
                vwt_ref[c * (TK // TQ) + h] = vw_ref[pl.ds(k0 + h * TQ, TQ), :].T.astype(BF16)
            return carry

        lax.fori_loop(0, seq // TK, prep, 0)

    q = q_ref[...]
    qf = jnp.concatenate([q[:, r * d:(r + 1) * d] for r in range(R)], axis=0)
    qs = qf.astype(BF16)
    qs2 = (qf * (scale * LOG2E)).astype(BF16)
    tcol = t0 + (lax.broadcasted_iota(jnp.int32, (1, NQ), 1) & (TQ - 1))

    ncp = kc_ref.shape[0]
    s1 = lax.dot_general(kc_ref[...].astype(BF16), qs, NT_DIMS, preferred_element_type=F32) * scale
    nid = lax.broadcasted_iota(jnp.int32, (ncp, NQ), 0)
    s1 = jnp.where(nid * NSA_CMP_STRIDE + (NSA_CMP_LEN - 1) <= tcol, s1, -jnp.inf)
    m1 = jnp.max(s1, axis=0, keepdims=True)
    m1 = jnp.where(m1 > -jnp.inf, m1, 0.0)
    e1 = jnp.exp(s1 - m1)
    z1 = jnp.sum(e1, axis=0, keepdims=True)
    p1 = e1 / jnp.where(z1 > 0, z1, 1.0)
    o_cmp = jnp.dot(vct_ref[...].astype(BF16), p1.astype(BF16), preferred_element_type=F32)

    p1sum = p1[:, 0:TQ]
    for r in range(1, R):
        p1sum = p1sum + p1[:, r * TQ:(r + 1) * TQ]
    imp = _dot_exact01(ovt_ref[...], p1sum)
    nblk = seq // NSA_SEL_LEN
    blk = lax.broadcasted_iota(jnp.int32, (LANES, TQ), 0)
    cur = (t0 + lax.broadcasted_iota(jnp.int32, (LANES, TQ), 1)) >> SEL_SHIFT
    imp = jnp.where(blk > cur, SEL_FUTURE, imp)
    imp = jnp.where((blk == 0) | (blk == cur) | (blk == cur - 1), SEL_FORCED, imp)
    imp = jnp.where(blk >= nblk, SEL_TAKEN, imp)
    sel = jnp.zeros((LANES, TQ), F32)
    for _ in range(min(NSA_N_SEL, nblk)):
        mx = jnp.max(imp, axis=0, keepdims=True)
        first = jnp.min(jnp.where(imp == mx, blk, LANES), axis=0, keepdims=True)
        hit = blk == first
        sel = jnp.where(hit, 1.0, sel)
        imp = jnp.where(hit, SEL_TAKEN, imp)
    bias = jnp.where(sel > 0, 0.0, SEL_MASK).T.astype(BF16)
    q_aug = jnp.concatenate([qs2, jnp.concatenate([bias] * R, axis=0)], axis=1)

    krow = lax.broadcasted_iota(jnp.int32, (TK, NQ), 0)
    acc_ref[...] = jnp.zeros_like(acc_ref)

    def sel_tile(j, ml, masked):
        m_old, l_old = ml
        s = lax.dot_general(kaug_ref[j], q_aug, NT_DIMS, preferred_element_type=F32)
        if masked:
            s = jnp.where(j * TK + krow <= tcol, s, -jnp.inf)
        m_new = jnp.maximum(m_old, jnp.max(s, axis=0, keepdims=True))
        alpha = jnp.exp2(m_old - m_new)
        p = jnp.exp2(s - m_new)
        l_new = alpha * l_old + jnp.sum(p, axis=0, keepdims=True)
        acc_ref[...] = alpha * acc_ref[...] + jnp.dot(vst_ref[j], p.astype(BF16),
                                                      preferred_element_type=F32)
        return m_new, l_new

    n_tiles = (t0 + TQ + TK - 1) // TK
    ml0 = (jnp.full((1, NQ), -jnp.inf, F32), jnp.zeros((1, NQ), F32))
    ml = lax.fori_loop(0, n_tiles - 1, lambda j, c: sel_tile(j, c, False), ml0)
    ml = sel_tile(n_tiles - 1, ml, True)
    o_slc = acc_ref[...] / ml[1]

    W = NSA_WINDOW
    span = min(W + TQ, seq)
    kstart = pl.multiple_of(jnp.maximum(t0 + TQ - span, 0), TQ)
    s3 = lax.dot_general(kwb_ref[pl.ds(kstart, span), :], qs2, NT_DIMS, preferred_element_type=F32)
    rel = tcol - kstart - lax.broadcasted_iota(jnp.int32, (span, NQ), 0)
    s3 = jnp.where((rel >= 0) & (rel < W), s3, -jnp.inf)
    e3 = jnp.exp2(s3 - jnp.max(s3, axis=0, keepdims=True))
    z3 = jnp.sum(e3, axis=0, keepdims=True)
    kb = kstart // TQ
    vwt = jnp.concatenate([vwt_ref[kb + h] for h in range(span // TQ)], axis=1)
    o_win = jnp.dot(vwt, e3.astype(BF16), preferred_element_type=F32) / z3

    gt = _sigmoid(gt_ref[...]).T
    gsel = jnp.where(g == 0, gt[0:3 * R], gt[3 * R:6 * R])
    for r in range(R):
        cols = slice(r * TQ, (r + 1) * TQ)
        o = (gsel[3 * r:3 * r + 1] * o_cmp[:, cols] + gsel[3 * r + 1:3 * r + 2] * o_slc[:, cols]
             + gsel[3 * r + 2:3 * r + 3] * o_win[:, cols])
        o_ref[:, r * d:(r + 1) * d] = o.T.astype(o_ref.dtype)


def nsa_attention(y3, k_cmp, v_cmp):
    batch, seq, n = y3.shape
    m = batch * seq
    TQ, TK, d = NSA_TQ, NSA_TK, NSA_DH
    R = NSA_HEADS // NSA_KV
    nq = seq // TQ
    ncp = k_cmp.shape[1]
    assert seq // NSA_SEL_LEN <= LANES and seq % TK == 0
    y2 = y3.reshape(m, n)
    v_cmp_t = v_cmp.transpose(0, 2, 1)
    nc = (seq - NSA_CMP_LEN) // NSA_CMP_STRIDE + 1
    cs = jnp.arange(ncp) * NSA_CMP_STRIDE
    ss = jnp.arange(LANES) * NSA_SEL_LEN
    overlap_t = ((cs[None, :] < ss[:, None] + NSA_SEL_LEN) & (cs[None, :] + NSA_CMP_LEN > ss[:, None])
                 & (jnp.arange(ncp)[None, :] < nc) & (jnp.arange(LANES)[:, None] < seq // NSA_SEL_LEN))
    overlap_t = overlap_t.astype(BF16)
    kv_spec = lambda col: pl.BlockSpec((None, seq, d), lambda b, g, i: (b, 0, col + g))
    return pl.pallas_call(
        functools.partial(_nsa_attn_kernel, seq=seq),
        out_shape=jax.ShapeDtypeStruct((m, MIX_HALF), BF16),
        grid=(batch, NSA_KV, nq),
        in_specs=[
            pl.BlockSpec((TQ, R * d), lambda b, g, i: (b * nq + i, g)),
            pl.BlockSpec((TQ, LANES), lambda b, g, i: (b * nq + i, CD_GATE_COL // LANES)),
            pl.BlockSpec((None, ncp, d), lambda b, g, i: (b, 0, g)),
            pl.BlockSpec((None, d, ncp), lambda b, g, i: (b, g, 0)),
            kv_spec(1536 // d), kv_spec(1792 // d), kv_spec(2048 // d), kv_spec(2304 // d),
            pl.BlockSpec((LANES, ncp), lambda b, g, i: (0, 0)),
        ],
        out_specs=pl.BlockSpec((TQ, R * d), lambda b, g, i: (b * nq + i, g)),
        scratch_shapes=[pltpu.VMEM((seq // TK, TK, d + LANES), BF16),
                        pltpu.VMEM((seq // TK, d, TK), BF16),
                        pltpu.VMEM((seq, d), BF16),
                        pltpu.VMEM((seq // TQ, d, TQ), BF16),
                        pltpu.VMEM((d, R * TQ), F32)],
        compiler_params=_params(("parallel", "parallel", "arbitrary")),
        name="nsa_attention",
    )(y2, y2, k_cmp, v_cmp_t, y3, y3, y3, y3, overlap_t)


def _router_kernel(x_ref, nw_ref, w2_ref, wh_ref, o_ref):
    x = x_ref[...]
    xn = x * lax.rsqrt(jnp.mean(x * x, axis=-1, keepdims=True) + RMS_EPS) * nw_ref[...]
    xh = xn.astype(BF16)
    xl = (xn - xh.astype(F32)).astype(BF16)
    r1 = jnp.dot(xh, w2_ref[...], preferred_element_type=F32)
    r2 = jnp.dot(xl, wh_ref[...], preferred_element_type=F32)
    o_ref[...] = r1[:, 0:LANES] + r1[:, LANES:] + r2


def router_logits(h, nw, w_grp, w_exp, tm):
    m, d = h.shape
    wr = jnp.zeros((d, LANES), F32).at[:, 0:MOE_GROUPS].set(w_grp)
    wr = wr.at[:, MOE_GROUPS:MOE_GROUPS + MOE_EXPERTS].set(w_exp)
    wh = wr.astype(BF16)
    wl = (wr - wh.astype(F32)).astype(BF16)
    return pl.pallas_call(
        _router_kernel,
        out_shape=jax.ShapeDtypeStruct((m, LANES), F32),
        grid=(m // tm,),
        in_specs=[
            pl.BlockSpec((tm, d), lambda i: (i, 0)),
            pl.BlockSpec((1, d), lambda i: (0, 0)),
            pl.BlockSpec((d, 2 * LANES), lambda i: (0, 0)),
            pl.BlockSpec((d, LANES), lambda i: (0, 0)),
        ],
        out_specs=pl.BlockSpec((tm, LANES), lambda i: (i, 0)),
        compiler_params=_params(("parallel",)),
        name="router_logits",
    )(h, nw.reshape(1, d), jnp.concatenate([wh, wl], axis=1), wh)


def _cast3_kernel(a_ref, b_ref, c_ref, oa_ref, ob_ref, oc_ref):
    oa_ref[...] = a_ref[...].astype(oa_ref.dtype)
    ob_ref[...] = b_ref[...].astype(ob_ref.dtype)
    oc_ref[...] = c_ref[...].astype(oc_ref.dtype)


def expert_weights_bf16(w_gate, w_up, w_down, layer):
    _, E, d, dff = w_gate.shape
    split = 2
    in_spec = lambda r, c: pl.BlockSpec((None, None, r // split, c), lambda e, s: (layer, e, s, 0))
    out_spec = lambda r, c: pl.BlockSpec((None, r // split, c), lambda e, s: (e, s, 0))
    return pl.pallas_call(
        _cast3_kernel,
        out_shape=(jax.ShapeDtypeStruct((E, d, dff), BF16), jax.ShapeDtypeStruct((E, d, dff), BF16),
                   jax.ShapeDtypeStruct((E, dff, d), BF16)),
        grid=(E, split),
        in_specs=[in_spec(d, dff), in_spec(d, dff), in_spec(dff, d)],
        out_specs=(out_spec(d, dff), out_spec(d, dff), out_spec(dff, d)),
        compiler_params=_params(("parallel", "parallel")),
        name="expert_weights_bf16",
    )(w_gate, w_up, w_down)


def _gather_rows(idx_ref, base, src_ref, dst_ref, sem, n_rows):
    def body(r, carry):
        tok = idx_ref[base + r]
        pltpu.make_async_copy(src_ref.at[pl.ds(tok, 1)], dst_ref.at[pl.ds(r, 1)], sem).start()
        return carry
    lax.fori_loop(0, n_rows, body, 0, unroll=8)


def _wait_rows(src_ref, dst_ref, sem, n_rows):
    pltpu.make_async_copy(src_ref.at[pl.ds(0, n_rows)], dst_ref, sem).wait()


def _expert_kernel(be_ref, nused_ref, tok_ref, h_ref, nw_ref, wg_ref, wu_ref, wd_ref, o_ref,
                   xbuf, sems):
    TM = MOE_TM
    i = pl.program_id(0)
    n_used = nused_ref[0]
    slot = i % 2

    @pl.when(i == 0)
    def _():
        _gather_rows(tok_ref, 0, h_ref, xbuf.at[0], sems.at[0], TM)

    @pl.when(i + 1 < n_used)
    def _():
        _gather_rows(tok_ref, (i + 1) * TM, h_ref, xbuf.at[1 - slot], sems.at[1 - slot], TM)

    @pl.when(i < n_used)
    def _():
        _wait_rows(h_ref, xbuf.at[slot], sems.at[slot], TM)
        x = xbuf[slot]
        xn = (x * lax.rsqrt(jnp.mean(x * x, axis=-1, keepdims=True) + RMS_EPS) * nw_ref[...]).astype(BF16)
        hg = jnp.dot(xn, wg_ref[...], preferred_element_type=F32)
        hu = jnp.dot(xn, wu_ref[...], preferred_element_type=F32)
        hb = (_silu(hg) * hu).astype(BF16)
        o_ref[...] = jnp.dot(hb, wd_ref[...], preferred_element_type=F32)

    @pl.when(i >= n_used)
    def _():
        o_ref[...] = jnp.zeros_like(o_ref)


def expert_ffn(h, nw, row_tok, blk_expert, n_used, w_gate, w_up, w_down):
    m, d = h.shape
    rows = row_tok.shape[0]
    TM = MOE_TM
    nb = rows // TM
    dff = w_gate.shape[2]
    grid_spec = pltpu.PrefetchScalarGridSpec(
        num_scalar_prefetch=3,
        grid=(nb,),
        in_specs=[
            pl.BlockSpec(memory_space=pl.ANY),
            pl.BlockSpec((1, d), lambda i, be, nu, tok: (0, 0)),
            pl.BlockSpec((None, d, dff), lambda i, be, nu, tok: (be[i], 0, 0)),
            pl.BlockSpec((None, d, dff), lambda i, be, nu, tok: (be[i], 0, 0)),
            pl.BlockSpec((None, dff, d), lambda i, be, nu, tok: (be[i], 0, 0)),
        ],
        out_specs=pl.BlockSpec((TM, d), lambda i, be, nu, tok: (i, 0)),
        scratch_shapes=[pltpu.VMEM((2, TM, d), F32), pltpu.SemaphoreType.DMA((2,))],
    )
    return pl.pallas_call(
        _expert_kernel,
        out_shape=jax.ShapeDtypeStruct((rows, d), F32),
        grid_spec=grid_spec,
        compiler_params=_params(("arbitrary",)),
        name="expert_ffn",
    )(blk_expert, n_used, row_tok, h, nw.reshape(1, d), w_gate, w_up, w_down)


def _combine_kernel(dest_ref, ys_ref, h_ref, gate_ref, fw_ref, o_ref, ybuf, sems, *, final_norm):
    tc = h_ref.shape[0]
    i = pl.program_id(0)
    n = pl.num_programs(0)
    slot = i % 2

    @pl.when(i == 0)
    def _():
        _gather_rows(dest_ref, 0, ys_ref, ybuf.at[0], sems.at[0], 2 * tc)

    @pl.when(i + 1 < n)
    def _():
        _gather_rows(dest_ref, (i + 1) * 2 * tc, ys_ref, ybuf.at[1 - slot], sems.at[1 - slot], 2 * tc)

    _wait_rows(ys_ref, ybuf.at[slot], sems.at[slot], 2 * tc)
    gate = gate_ref[...]
    out = h_ref[...] + (ybuf[slot, 0:tc, :] * gate[:, 0:1] + ybuf[slot, tc:, :] * gate[:, 1:2])
    if final_norm:
        out = out * lax.rsqrt(jnp.mean(out * out, axis=-1, keepdims=True) + RMS_EPS) * fw_ref[...]
    o_ref[...] = out


def moe_combine(h, ys, dest_tiles, gate, final_w, tc, final_norm):
    m, d = h.shape
    grid_spec = pltpu.PrefetchScalarGridSpec(
        num_scalar_prefetch=1,
        grid=(m // tc,),
        in_specs=[
            pl.BlockSpec(memory_space=pl.ANY),
            pl.BlockSpec((tc, d), lambda i, dst: (i, 0)),
            pl.BlockSpec((tc, MOE_TOPK), lambda i, dst: (i, 0)),
            pl.BlockSpec((1, d), lambda i, dst: (0, 0)),
        ],
        out_specs=pl.BlockSpec((tc, d), lambda i, dst: (i, 0)),
        scratch_shapes=[pltpu.VMEM((2, 2 * tc, d), F32), pltpu.SemaphoreType.DMA((2,))],
    )
    return pl.pallas_call(
        functools.partial(_combine_kernel, final_norm=final_norm),
        out_shape=jax.ShapeDtypeStruct((m, d), F32),
        grid_spec=grid_spec,
        compiler_params=_params(("arbitrary",)),
        name="moe_combine",
    )(dest_tiles, ys, h, gate, final_w.reshape(1, d))


def _route(logits, b_grp, b_exp):
    n = logits.shape[0]
    K, E, TM = MOE_TOPK, MOE_EXPERTS, MOE_TM
    grp_logits = logits[:, 0:MOE_GROUPS] + b_grp.astype(F32)
    grp_prob = jax.nn.softmax(grp_logits, axis=-1)
    g_idx = jnp.argmax(grp_logits, axis=-1)
    p_grp = jnp.take_along_axis(grp_prob, g_idx[:, None], axis=1)[:, 0]
    exp_logits = (logits[:, MOE_GROUPS:MOE_GROUPS + E] + b_exp.astype(F32)).reshape(n, MOE_GROUPS, MOE_EPG)
    within = jnp.take_along_axis(exp_logits, g_idx[:, None, None], axis=1)[:, 0]
    top_p, top_i = lax.top_k(jax.nn.softmax(within, axis=-1), K)
    gate = p_grp[:, None] * top_p / jnp.sum(top_p, axis=-1, keepdims=True)
    expert = (g_idx[:, None] * MOE_EPG + top_i).astype(jnp.int32)

    A = n * K
    flat_e = expert.reshape(A)
    iota = jnp.arange(A, dtype=jnp.int32)
    e_sorted, order = lax.sort((flat_e, iota), num_keys=1)
    counts = jnp.sum((flat_e[:, None] == jnp.arange(E, dtype=jnp.int32)[None, :]).astype(jnp.int32), axis=0)
    padded = (counts + TM - 1) // TM * TM
    pad_end = jnp.cumsum(padded)
    pad_start = pad_end - padded
    start = jnp.cumsum(counts) - counts
    dest_sorted = (pad_start[e_sorted] + iota - start[e_sorted]).astype(jnp.int32)
    _, dest = lax.sort((order, dest_sorted), num_keys=1)
    rows = A + E * TM
    nb = rows // TM
    blk_expert = jnp.minimum(jnp.sum((jnp.arange(nb, dtype=jnp.int32)[:, None] * TM >= pad_end[None, :])
                                     .astype(jnp.int32), axis=1), E - 1).astype(jnp.int32)
    r = jnp.arange(rows, dtype=jnp.int32)
    e_row = jnp.repeat(blk_expert, TM)
    pos = r - pad_start[e_row].astype(jnp.int32)
    src = jnp.clip(start[e_row].astype(jnp.int32) + pos, 0, A - 1)
    row_tok = jnp.where(pos < counts[e_row], order[src] // K, 0).astype(jnp.int32)
    n_used = (pad_end[-1] // TM).astype(jnp.int32).reshape(1)
    return gate.astype(F32), dest.reshape(n, K), row_tok, blk_expert, n_used


def hier_moe(h, nw, w_grp, b_grp, w_exp, b_exp, w_gate_all, w_up_all, w_down_all, layer, final_w,
             final_norm):
    m, d = h.shape
    tc = 256
    logits = router_logits(h, nw, w_grp, w_exp, 512)
    gate, dest, row_tok, blk_expert, n_used = _route(logits, b_grp, b_exp)
    w_gate, w_up, w_down = expert_weights_bf16(w_gate_all, w_up_all, w_down_all, layer)
    ys = expert_ffn(h, nw, row_tok, blk_expert, n_used, w_gate, w_up, w_down)
    dest_tiles = dest.reshape(m // tc, tc, MOE_TOPK).transpose(0, 2, 1).reshape(-1)
    return moe_combine(h, ys, dest_tiles, gate, final_w, tc, final_norm)


def _cd_in_weights(w):
    gates = w[:, 2560:2584]
    pad = jnp.zeros((w.shape[0], CD_N - CD_GATE_COL - gates.shape[1]), w.dtype)
    return jnp.concatenate([w[:, 0:2560], w[:, 2584:], gates, pad], axis=1)


def mixer_ab(h, nw, w_in, w_out, lb, hg_norm, sw_sinks, batch, seq):
    y = norm_matmul(h, nw, w_in.astype(BF16), 512, 1408)
    o_a = hgrn2(y, lb, hg_norm, batch, seq, min(512, seq))
    o_b = swa(y, sw_sinks, batch, seq)
    return outproj_residual(o_a, o_b, w_out.astype(BF16), h, 512)


def mixer_cd(h, nw, w_in, w_out, pe_k, w1_k, w2_k, pe_v, w1_v, w2_v, ret_norm, batch, seq):
    y = norm_matmul(h, nw, _cd_in_weights(w_in).astype(BF16), 512, 1920)
    y3 = y.reshape(batch, seq, CD_N)
    k_cmp, v_cmp = nsa_compress(y3, pe_k, w1_k, w2_k, pe_v, w1_v, w2_v)
    o_c = nsa_attention(y3, k_cmp, v_cmp)
    o_d = retention(y, ret_norm, batch, seq, min(512, seq))
    return outproj_residual(o_c, o_d, w_out.astype(BF16), h, 512)


def kernel(x, norm_mix, norm_ffn, norm_final, ab_w_in, ab_w_out, hg_lb_logits, hg_norm, sw_sinks, cd_w_in, cd_w_out, nsa_pe_k, nsa_w1_k, nsa_w2_k, nsa_pe_v, nsa_w1_v, nsa_w2_v, ret_norm, moe_w_grp, moe_b_grp, moe_w_exp, moe_b_exp, moe_w_gate, moe_w_up, moe_w_down):
    batch, seq, d = x.shape
    depth = norm_mix.shape[0]
    lb_all = jnp.cumsum(jax.nn.softmax(hg_lb_logits.astype(F32), axis=0), axis=0)
    h = x.reshape(batch * seq, d)
    for layer in range(depth):
        if layer % 2 == 0:
            e = layer // 2
            h = mixer_ab(h, norm_mix[layer], ab_w_in[e], ab_w_out[e], lb_all[layer], hg_norm[e],
                         sw_sinks[e], batch, seq)
        else:
            o = layer // 2
            h = mixer_cd(h, norm_mix[layer], cd_w_in[o], cd_w_out[o], nsa_pe_k[o], nsa_w1_k[o],
                         nsa_w2_k[o], nsa_pe_v[o], nsa_w1_v[o], nsa_w2_v[o], ret_norm[o], batch, seq)
        h = hier_moe(h, norm_ffn[layer], moe_w_grp[layer], moe_b_grp[layer], moe_w_exp[layer],
                     moe_b_exp[layer], moe_w_gate, moe_w_up, moe_w_down, layer,
                     norm_final, layer == depth - 1)
    return h.reshape(batch, seq, d)
```

```python
import functools

import jax
import jax.numpy as jnp
from jax import lax
from jax.experimental import pallas as pl
from jax.experimental.pallas import tpu as pltpu

F32 = jnp.float32
BF16 = jnp.bfloat16

D_MODEL = 2048
MIX_HALF = D_MODEL // 2
RMS_EPS = 1e-6
LANES = 128
VMEM_LIMIT = 56 * 1024 * 1024

HG_DK = 128
HG_HEADS = 8
HG_CHUNK = 64
HG_SUB = 16
SW_DH = 128
SW_HEADS = 8
SW_KV = 2
SW_WINDOW = 128
NSA_DH = 128
NSA_HEADS = 8
NSA_KV = 2
NSA_CMP_LEN = 32
NSA_CMP_STRIDE = 16
NSA_CMP_HIDDEN = 128
NSA_SEL_LEN = 64
NSA_N_SEL = 8
NSA_WINDOW = 512
RET_DK = 128
RET_DV = 256
RET_HEADS = 4
RET_CHUNK = 128
RET_THETA_BASE = 10000.0
MOE_GROUPS = 4
MOE_EPG = 8
MOE_EXPERTS = 32
MOE_TOPK = 2
MOE_DFF = 1024
MOE_TM = 256

AB_N = 5632
CD_N = 5760
CD_GATE_COL = 5632

NT_DIMS = (((1,), (1,)), ((), ()))
TN_DIMS = (((0,), (0,)), ((), ()))


def _params(sem, vmem=VMEM_LIMIT):
    return pltpu.CompilerParams(dimension_semantics=sem, vmem_limit_bytes=vmem)


def _sigmoid(x):
    return 1.0 / (1.0 + jnp.exp(-x))


def _silu(x):
    return x * _sigmoid(x)


def _dot_exact01(a01, x):
    n = x.shape[1]
    hi = x.astype(BF16)
    rem = x - hi.astype(F32)
    mid = rem.astype(BF16)
    lo = (rem - mid.astype(F32)).astype(BF16)
    r = jnp.dot(a01, jnp.concatenate([hi, mid, lo], axis=1), preferred_element_type=F32)
    return r[:, 0:n] + r[:, n:2 * n] + r[:, 2 * n:]


def _norm_matmul_kernel(x_ref, nw_ref, w_ref, o_ref, xn_ref):
    @pl.when(pl.program_id(1) == 0)
    def _():
        x = x_ref[...]
        ms = jnp.mean(x * x, axis=-1, keepdims=True)
        xn_ref[...] = (x * lax.rsqrt(ms + RMS_EPS) * nw_ref[...]).astype(BF16)

    o_ref[...] = jnp.dot(xn_ref[...], w_ref[...], preferred_element_type=F32)


def norm_matmul(x, nw, w, tm, tn):
    m, d = x.shape
    n = w.shape[1]
    return pl.pallas_call(
        _norm_matmul_kernel,
        out_shape=jax.ShapeDtypeStruct((m, n), F32),
        grid=(m // tm, n // tn),
        in_specs=[
            pl.BlockSpec((tm, d), lambda i, j: (i, 0)),
            pl.BlockSpec((1, d), lambda i, j: (0, 0)),
            pl.BlockSpec((d, tn), lambda i, j: (0, j)),
        ],
        out_specs=pl.BlockSpec((tm, tn), lambda i, j: (i, j)),
        scratch_shapes=[pltpu.VMEM((tm, d), BF16)],
        compiler_params=_params(("parallel", "arbitrary")),
        name="norm_matmul",
    )(x, nw.reshape(1, d), w)


def _outproj_kernel(a_ref, b_ref, w_ref, h_ref, o_ref):
    half = a_ref.shape[1]
    acc = jnp.dot(a_ref[...], w_ref[0:half, :], preferred_element_type=F32)
    acc = acc + jnp.dot(b_ref[...], w_ref[half:, :], preferred_element_type=F32)
    o_ref[...] = h_ref[...] + acc


def outproj_residual(a, b, w, h, tm):
    m, half = a.shape
    d = w.shape[1]
    return pl.pallas_call(
        _outproj_kernel,
        out_shape=jax.ShapeDtypeStruct((m, d), F32),
        grid=(m // tm,),
        in_specs=[
            pl.BlockSpec((tm, half), lambda i: (i, 0)),
            pl.BlockSpec((tm, half), lambda i: (i, 0)),
            pl.BlockSpec((2 * half, d), lambda i: (0, 0)),
            pl.BlockSpec((tm, d), lambda i: (i, 0)),
        ],
        out_specs=pl.BlockSpec((tm, d), lambda i: (i, 0)),
        compiler_params=_params(("parallel",)),
        name="outproj_residual",
    )(a, b, w, h)


def _hgrn2_kernel(q_ref, f_ref, i_ref, g_ref, lb_ref, nw_ref, o_ref, st_ref, *, n_chunks):
    C, c = HG_CHUNK, HG_SUB
    nsub = C // c

    @pl.when(pl.program_id(2) == 0)
    def _():
        st_ref[...] = jnp.zeros_like(st_ref)

    lb = lb_ref[...]
    nw = nw_ref[...]
    tri = (lax.broadcasted_iota(jnp.int32, (C, C), 0)
           >= lax.broadcasted_iota(jnp.int32, (C, C), 1)).astype(BF16)
    hs = c // 2
    row8 = lax.broadcasted_iota(jnp.int32, (1, hs, 1), 1)
    chunks = range(n_chunks)
    rows_of = lambda ci: slice(ci * C, (ci + 1) * C)

    q = [q_ref[rows_of(ci), :] for ci in chunks]
    v = [i_ref[rows_of(ci), :] for ci in chunks]
    vb = [x.astype(BF16) for x in v]
    f = [lb + (1.0 - lb) * _sigmoid(f_ref[rows_of(ci), :]) for ci in chunks]
    k = [1.0 - x for x in f]
    b = [_dot_exact01(tri, jnp.log(x)) for x in f]

    a_off = []
    for ci in chunks:
        for i in range(1, nsub):
            r = b[ci][i * c - 1:i * c, :]
            qi = q[ci][i * c:(i + 1) * c] * jnp.exp(b[ci][i * c:(i + 1) * c] - r)
            ki = k[ci][0:i * c] * jnp.exp(r - b[ci][0:i * c])
            a_off.append(lax.dot_general(qi.astype(BF16), ki.astype(BF16), NT_DIMS,
                                         preferred_element_type=F32))
    o_off = []
    for ci in chunks:
        rows = [jnp.zeros((c, HG_DK), F32)]
        for i in range(1, nsub):
            a = a_off[ci * (nsub - 1) + i - 1]
            rows.append(jnp.dot(a.astype(BF16), vb[ci][0:i * c], preferred_element_type=F32))
        o_off.append(jnp.concatenate(rows, axis=0))
    inc = []
    for ci in chunks:
        bend = b[ci][C - 1:C, :]
        kd = k[ci] * jnp.exp(bend - b[ci])
        inc.append(lax.dot_general(vb[ci], kd.astype(BF16), TN_DIMS, preferred_element_type=F32))
    st = st_ref[...]
    o_st = []
    for ci in chunks:
        o_st.append(lax.dot_general((q[ci] * jnp.exp(b[ci])).astype(BF16), st.astype(BF16), NT_DIMS,
                                    preferred_element_type=F32))
        st = st * jnp.exp(b[ci][C - 1:C, :]) + inc[ci]
    st_ref[...] = st

    for ci in chunks:
        b4 = b[ci].reshape(nsub, c, HG_DK)
        q4 = q[ci].reshape(nsub, c, HG_DK)
        k4 = k[ci].reshape(nsub, c, HG_DK)
        v4 = v[ci].reshape(nsub, c, HG_DK)
        bt, bb = b4[:, 0:hs], b4[:, hs:]
        qt, qb = q4[:, 0:hs], q4[:, hs:]
        od_t = jnp.zeros((nsub, hs, HG_DK), F32)
        od_b = jnp.zeros((nsub, hs, HG_DK), F32)
        for s in range(c):
            piv, kp, vp = b4[:, s:s + 1], k4[:, s:s + 1], v4[:, s:s + 1]
            if s < hs:
                e = jnp.exp(jnp.minimum(bt - piv, 0.0))
                a = jnp.sum(qt * e * kp, axis=-1, keepdims=True)
                od_t = od_t + jnp.where(row8 >= s, a, 0.0) * vp
                e = jnp.exp(jnp.minimum(bb - piv, 0.0))
                od_b = od_b + jnp.sum(qb * e * kp, axis=-1, keepdims=True) * vp
            else:
                e = jnp.exp(jnp.minimum(bb - piv, 0.0))
                a = jnp.sum(qb * e * kp, axis=-1, keepdims=True)
                od_b = od_b + jnp.where(row8 >= s - hs, a, 0.0) * vp
        o = o_off[ci] + o_st[ci] + jnp.concatenate([od_t, od_b], axis=1).reshape(C, HG_DK)
        y = o * lax.rsqrt(jnp.mean(o * o, axis=-1, keepdims=True) + RMS_EPS) * nw
        o_ref[rows_of(ci), :] = (y * _silu(g_ref[rows_of(ci), :])).astype(o_ref.dtype)


def hgrn2(y, lb, nw, batch, seq, tc):
    m = y.shape[0]
    nt = seq // tc
    row = lambda b, h, t: b * nt + t
    hd = HG_HEADS
    return pl.pallas_call(
        functools.partial(_hgrn2_kernel, n_chunks=tc // HG_CHUNK),
        out_shape=jax.ShapeDtypeStruct((m, MIX_HALF), BF16),
        grid=(batch, hd, nt),
        in_specs=[
            pl.BlockSpec((tc, HG_DK), lambda b, h, t: (row(b, h, t), h)),
            pl.BlockSpec((tc, HG_DK), lambda b, h, t: (row(b, h, t), hd + h)),
            pl.BlockSpec((tc, HG_DK), lambda b, h, t: (row(b, h, t), 2 * hd + h)),
            pl.BlockSpec((tc, HG_DK), lambda b, h, t: (row(b, h, t), 3 * hd + h)),
            pl.BlockSpec((None, 1, HG_DK), lambda b, h, t: (h, 0, 0)),
            pl.BlockSpec((None, 1, HG_DK), lambda b, h, t: (h, 0, 0)),
        ],
        out_specs=pl.BlockSpec((tc, HG_DK), lambda b, h, t: (row(b, h, t), h)),
        scratch_shapes=[pltpu.VMEM((HG_DK, HG_DK), F32)],
        compiler_params=_params(("parallel", "parallel", "arbitrary")),
        name="hgrn2",
    )(y, y, y, y, lb.reshape(hd, 1, HG_DK), nw.reshape(hd, 1, HG_DK))


def _swa_kernel(q_ref, kp_ref, kc_ref, vp_ref, vc_ref, sink_ref, o_ref):
    W, d = SW_WINDOW, SW_DH
    G = SW_KV
    R = SW_HEADS // G
    NQ = R * W
    groups = range(G)
    n = pl.program_id(1)
    q = q_ref[...]
    qs = [(jnp.concatenate([q[:, (g * R + r) * d:(g * R + r + 1) * d] for r in range(R)], axis=0)
           * (d ** -0.5 * LOG2E)).astype(BF16) for g in groups]
    kk = [jnp.concatenate([kp_ref[:, g * d:(g + 1) * d], kc_ref[:, g * d:(g + 1) * d]],
                          axis=0).astype(BF16) for g in groups]
    vt = [jnp.concatenate([vp_ref[:, g * d:(g + 1) * d].T, vc_ref[:, g * d:(g + 1) * d].T],
                          axis=1).astype(BF16) for g in groups]
    s = [lax.dot_general(kk[g], qs[g], NT_DIMS, preferred_element_type=F32) for g in groups]
    kj = lax.broadcasted_iota(jnp.int32, (2 * W, NQ), 0)
    rel = (lax.broadcasted_iota(jnp.int32, (2 * W, NQ), 1) & (W - 1)) + W - kj
    valid = (rel >= 0) & (rel < W) & ((kj >= W) | (n > 0))
    outs = []
    for g in groups:
        sg = jnp.where(valid, s[g], -jnp.inf)
        sink = sink_ref[g] * LOG2E
        mx = jnp.maximum(jnp.max(sg, axis=0, keepdims=True), sink)
        p = jnp.exp2(sg - mx)
        z = jnp.sum(p, axis=0, keepdims=True) + jnp.exp2(sink - mx)
        outs.append(jnp.dot(vt[g], p.astype(BF16), preferred_element_type=F32) / z)
    for g in groups:
        for r in range(R):
            o_ref[:, (g * R + r) * d:(g * R + r + 1) * d] = outs[g][:, r * W:(r + 1) * W].T.astype(o_ref.dtype)


def swa(y, sinks, batch, seq):
    m = y.shape[0]
    W, d, G = SW_WINDOW, SW_DH, SW_KV
    R = SW_HEADS // G
    nb = seq // W
    qcol = 4 * MIX_HALF // (SW_HEADS * d)
    kcol = (4 * MIX_HALF + SW_HEADS * d) // (G * d)
    vcol = kcol + 1
    cur = lambda b, n: b * nb + n
    prev = lambda b, n: b * nb + jnp.maximum(n - 1, 0)
    sink_cols = jnp.repeat(sinks.astype(F32).reshape(G, R), W, axis=1).reshape(G, 1, R * W)
    return pl.pallas_call(
        _swa_kernel,
        out_shape=jax.ShapeDtypeStruct((m, MIX_HALF), BF16),
        grid=(batch, nb),
        in_specs=[
            pl.BlockSpec((W, SW_HEADS * d), lambda b, n: (cur(b, n), qcol)),
            pl.BlockSpec((W, G * d), lambda b, n: (prev(b, n), kcol)),
            pl.BlockSpec((W, G * d), lambda b, n: (cur(b, n), kcol)),
            pl.BlockSpec((W, G * d), lambda b, n: (prev(b, n), vcol)),
            pl.BlockSpec((W, G * d), lambda b, n: (cur(b, n), vcol)),
            pl.BlockSpec((G, 1, R * W), lambda b, n: (0, 0, 0)),
        ],
        out_specs=pl.BlockSpec((W, SW_HEADS * d), lambda b, n: (cur(b, n), 0)),
        compiler_params=_params(("parallel", "parallel")),
        name="swa",
    )(y, y, y, y, y, sink_cols)


def _retention_kernel(q_ref, k_ref, v_ref, g_ref, cos_ref, sin_ref, dmat_ref, xi_ref, zeta_ref,
                      cdec_ref, nw_ref, o_ref, st_ref, *, n_chunks):
    C = RET_CHUNK

    @pl.when(pl.program_id(2) == 0)
    def _():
        st_ref[...] = jnp.zeros_like(st_ref)

    dmat = dmat_ref[...]
    xi = xi_ref[...]
    zeta = zeta_ref[...]
    cdec = cdec_ref[...]
    nw = nw_ref[...]

    chunks = range(n_chunks)
    rows_of = lambda ci: slice(ci * C, (ci + 1) * C)

    def rotate(x_ref, ci):
        x = x_ref[rows_of(ci), :]
        return x * cos_ref[rows_of(ci), :] + pltpu.roll(x, RET_DK // 2, 1) * sin_ref[rows_of(ci), :]

    qr = [rotate(q_ref, ci) for ci in chunks]
    kr = [rotate(k_ref, ci) * (RET_DK ** -0.5) for ci in chunks]
    vb = [v_ref[rows_of(ci), :].astype(BF16) for ci in chunks]
    inner = [lax.dot_general(qr[ci].astype(BF16), kr[ci].astype(BF16), NT_DIMS,
                             preferred_element_type=F32) * dmat for ci in chunks]
    inc = [lax.dot_general((kr[ci] * zeta).astype(BF16), vb[ci], TN_DIMS,
                           preferred_element_type=F32) for ci in chunks]
    o_in = [jnp.dot(inner[ci].astype(BF16), vb[ci], preferred_element_type=F32) for ci in chunks]
    st = st_ref[...]
    o_cross = []
    for ci in chunks:
        o_cross.append(jnp.dot((qr[ci] * xi).astype(BF16), st.astype(BF16), preferred_element_type=F32))
        st = cdec * st + inc[ci]
    st_ref[...] = st
    for ci in chunks:
        o = o_in[ci] + o_cross[ci]
        cen = o - jnp.mean(o, axis=-1, keepdims=True)
        y = cen * lax.rsqrt(jnp.mean(cen * cen, axis=-1, keepdims=True) + RMS_EPS) * nw
        o_ref[rows_of(ci), :] = (y * _silu(g_ref[rows_of(ci), :])).astype(o_ref.dtype)


def retention(y, nw, batch, seq, tc):
    m = y.shape[0]
    nt = seq // tc
    H, C = RET_HEADS, RET_CHUNK
    log_g = jnp.log1p(-jnp.exp2(-5.0 - jnp.arange(H, dtype=F32)))
    j = jnp.arange(C, dtype=F32)
    diff = j[:, None] - j[None, :]
    dmat = jnp.where(diff >= 0, jnp.exp(jnp.maximum(diff, 0.0)[None] * log_g[:, None, None]), 0.0)
    zeta = jnp.exp((C - 1 - j)[None, :] * log_g[:, None]).reshape(H, C, 1)
    xi = jnp.exp((j + 1.0)[None, :] * log_g[:, None]).reshape(H, C, 1)
    cdec = jnp.broadcast_to(jnp.exp(C * log_g)[:, None, None], (H, 1, RET_DV))
    half = RET_DK // 2
    theta = 1.0 / (RET_THETA_BASE ** jnp.linspace(0.0, 1.0, half, dtype=F32))
    ang = jnp.arange(seq).astype(F32)[:, None] * theta[None, :]
    cos2 = jnp.concatenate([jnp.cos(ang), jnp.cos(ang)], axis=-1)
    sin2 = jnp.concatenate([-jnp.sin(ang), jnp.sin(ang)], axis=-1)

    row = lambda b, h, t: b * nt + t
    qcol = 2560 // RET_DK
    kcol = 3072 // RET_DK
    vcol = 3584 // RET_DV
    gcol = 4608 // RET_DV
    return pl.pallas_call(
        functools.partial(_retention_kernel, n_chunks=tc // C),
        out_shape=jax.ShapeDtypeStruct((m, MIX_HALF), BF16),
        grid=(batch, H, nt),
        in_specs=[
            pl.BlockSpec((tc, RET_DK), lambda b, h, t: (row(b, h, t), qcol + h)),
            pl.BlockSpec((tc, RET_DK), lambda b, h, t: (row(b, h, t), kcol + h)),
            pl.BlockSpec((tc, RET_DV), lambda b, h, t: (row(b, h, t), vcol + h)),
            pl.BlockSpec((tc, RET_DV), lambda b, h, t: (row(b, h, t), gcol + h)),
            pl.BlockSpec((tc, RET_DK), lambda b, h, t: (t, 0)),
            pl.BlockSpec((tc, RET_DK), lambda b, h, t: (t, 0)),
            pl.BlockSpec((None, C, C), lambda b, h, t: (h, 0, 0)),
            pl.BlockSpec((None, C, 1), lambda b, h, t: (h, 0, 0)),
            pl.BlockSpec((None, C, 1), lambda b, h, t: (h, 0, 0)),
            pl.BlockSpec((None, 1, RET_DV), lambda b, h, t: (h, 0, 0)),
            pl.BlockSpec((None, 1, RET_DV), lambda b, h, t: (h, 0, 0)),
        ],
        out_specs=pl.BlockSpec((tc, RET_DV), lambda b, h, t: (row(b, h, t), h)),
        scratch_shapes=[pltpu.VMEM((RET_DK, RET_DV), F32)],
        compiler_params=_params(("parallel", "parallel", "arbitrary")),
        name="retention",
    )(y, y, y, y, cos2, sin2, dmat, xi, zeta, cdec, nw.reshape(H, 1, RET_DV))


def _gelu_tanh(x):
    return 0.5 * x * (1.0 + jnp.tanh(0.7978845608028654 * (x + 0.044715 * x * x * x)))


def _nsa_compress_kernel(xk_ref, xv_ref, pek_ref, pev_ref, w1k_ref, w1v_ref, w2k_ref, w2v_ref,
                         ok_ref, ov_ref):
    gh = NSA_KV * NSA_CMP_HIDDEN

    def one(x_ref, pe_ref, w1_ref, w2_ref, o_ref):
        w1 = w1_ref[...]
        pq = jnp.dot(x_ref[...].astype(BF16), w1, preferred_element_type=F32)
        pe = jnp.dot(pe_ref[...].astype(BF16), w1, preferred_element_type=F32)
        const = pe[0:1, 0:gh] + pe[1:2, gh:]
        nrow = pq.shape[0]
        nxt = pltpu.roll(pq[:, gh:], nrow - 1, 0)
        hid = _gelu_tanh(pq[:, 0:gh] + nxt + const)
        o_ref[...] = jnp.dot(hid.astype(BF16), w2_ref[...], preferred_element_type=F32)

    one(xk_ref, pek_ref, w1k_ref, w2k_ref, ok_ref)
    one(xv_ref, pev_ref, w1v_ref, w2v_ref, ov_ref)


def _compress_weights(pe, w1, w2):
    G, L, S = NSA_KV, NSA_CMP_LEN, NSA_CMP_STRIDE
    d, hd = NSA_DH, NSA_CMP_HIDDEN
    eye = jnp.eye(G, dtype=F32)

    def big(w):
        return jnp.einsum('ldh,ge->lgdeh', w, eye).reshape(S * G * d, G * hd)

    w1big = jnp.concatenate([big(w1[:S]), big(w1[S:])], axis=1).astype(BF16)
    w2big = jnp.einsum('hd,ge->ghed', w2, eye).reshape(G * hd, G * d).astype(BF16)

    def pe_row(p):
        return jnp.broadcast_to(p[:, None, :], (S, G, d)).reshape(S * G * d)

    pe8 = jnp.zeros((8, S * G * d), F32).at[0].set(pe_row(pe[:S])).at[1].set(pe_row(pe[S:]))
    return pe8, w1big, w2big


def nsa_compress(y3, pe_k, w1_k, w2_k, pe_v, w1_v, w2_v):
    batch, seq, _ = y3.shape
    G, S, d = NSA_KV, NSA_CMP_STRIDE, NSA_DH
    nrow = seq // S
    xk = y3[:, :, 1024:1280].reshape(batch, nrow, S * G * d)
    xv = y3[:, :, 1280:1536].reshape(batch, nrow, S * G * d)
    pek, w1k, w2k = _compress_weights(pe_k, w1_k, w2_k)
    pev, w1v, w2v = _compress_weights(pe_v, w1_v, w2_v)
    kdim = S * G * d
    full = lambda shape: pl.BlockSpec(shape, lambda b: (0,) * len(shape))
    out = jax.ShapeDtypeStruct((batch, nrow, G * d), F32)
    return pl.pallas_call(
        _nsa_compress_kernel,
        out_shape=(out, out),
        grid=(batch,),
        in_specs=[
            pl.BlockSpec((None, nrow, kdim), lambda b: (b, 0, 0)),
            pl.BlockSpec((None, nrow, kdim), lambda b: (b, 0, 0)),
            full((8, kdim)), full((8, kdim)),
            full(w1k.shape), full(w1v.shape), full(w2k.shape), full(w2v.shape),
        ],
        out_specs=(pl.BlockSpec((None, nrow, G * d), lambda b: (b, 0, 0)),
                   pl.BlockSpec((None, nrow, G * d), lambda b: (b, 0, 0))),
        compiler_params=_params(("parallel",)),
        name="nsa_compress",
    )(xk, xv, pek, pev, w1k, w1v, w2k, w2v)


NSA_TQ = 128
NSA_TK = 512
SEL_SHIFT = 6
SEL_FORCED = 1e30
SEL_FUTURE = -1e30
SEL_TAKEN = -3e38
SEL_MASK = -1e9
LOG2E = 1.4426950408889634


def _nsa_prep_kernel(ks_ref, vs_ref, kw_ref, vw_ref, kaug_ref, vst_ref, kwb_ref, vwt_ref):
    TK, TQ = NSA_TK, NSA_TQ
    k0 = pl.program_id(2) * TK
    krow = lax.broadcasted_iota(jnp.int32, (TK, LANES), 0)
    bcol = lax.broadcasted_iota(jnp.int32, (TK, LANES), 1)
    onehot = jnp.where(((k0 + krow) >> SEL_SHIFT) == bcol, 1.0, 0.0).astype(BF16)
    kaug_ref[...] = jnp.concatenate([ks_ref[...].astype(BF16), onehot], axis=1)
    vst_ref[...] = vs_ref[...].T.astype(BF16)
    kwb_ref[...] = kw_ref[...].astype(BF16)
    for h in range(TK // TQ):
        vwt_ref[h] = vw_ref[h * TQ:(h + 1) * TQ, :].T.astype(BF16)


def nsa_prep(y3):
    batch, seq, _ = y3.shape
    G, d, TK, TQ = NSA_KV, NSA_DH, NSA_TK, NSA_TQ
    nk = seq // TK
    src = lambda col: pl.BlockSpec((None, TK, d), lambda b, g, c: (b, c, col + g))
    return pl.pallas_call(
        _nsa_prep_kernel,
        out_shape=(jax.ShapeDtypeStruct((batch, G, nk, TK, d + LANES), BF16),
                   jax.ShapeDtypeStruct((batch, G, nk, d, TK), BF16),
                   jax.ShapeDtypeStruct((batch, G, seq, d), BF16),
                   jax.ShapeDtypeStruct((batch, G, seq // TQ, d, TQ), BF16)),
        grid=(batch, G, nk),
        in_specs=[src(1536 // d), src(1792 // d), src(2048 // d), src(2304 // d)],
        out_specs=(pl.BlockSpec((None, None, None, TK, d + LANES), lambda b, g, c: (b, g, c, 0, 0)),
                   pl.BlockSpec((None, None, None, d, TK), lambda b, g, c: (b, g, c, 0, 0)),
                   pl.BlockSpec((None, None, TK, d), lambda b, g, c: (b, g, c, 0)),
                   pl.BlockSpec((None, None, TK // TQ, d, TQ), lambda b, g, c: (b, g, c, 0, 0))),
        compiler_params=_params(("parallel", "parallel", "parallel")),
        name="nsa_prep",
    )(y3, y3, y3, y3)


def _nsa_attn_kernel(q_ref, gt_ref, kc_ref, vct_ref, kaug_ref, vst_ref, kwb_ref, vwt_ref, ovt_ref,
                     o_ref, acc_ref, *, seq):
    TQ, TK, d = NSA_TQ, NSA_TK, NSA_DH
    G = NSA_KV
    R = NSA_HEADS // G
    NQ = R * TQ
    groups = range(G)
    i = pl.program_id(1)
    t0 = i * TQ
    scale = d ** -0.5
    tcol = t0 + (lax.broadcasted_iota(jnp.int32, (1, NQ), 1) & (TQ - 1))

    q = q_ref[...]
    qf = [jnp.concatenate([q[:, (g * R + r) * d:(g * R + r + 1) * d] for r in range(R)], axis=0)
          for g in groups]
    qs = [x.astype(BF16) for x in qf]
    qs2 = [(x * (scale * LOG2E)).astype(BF16) for x in qf]

    ncp = kc_ref.shape[0]
    nid = lax.broadcasted_iota(jnp.int32, (ncp, NQ), 0)
    cmp_ok = nid * NSA_CMP_STRIDE + (NSA_CMP_LEN - 1) <= tcol
    s1 = [lax.dot_general(kc_ref[:, g * d:(g + 1) * d].astype(BF16), qs[g], NT_DIMS,
                          preferred_element_type=F32) * scale for g in groups]
    p1 = []
    for g in groups:
        s = jnp.where(cmp_ok, s1[g], -jnp.inf)
        m1 = jnp.max(s, axis=0, keepdims=True)
        m1 = jnp.where(m1 > -jnp.inf, m1, 0.0)
        e1 = jnp.exp(s - m1)
        z1 = jnp.sum(e1, axis=0, keepdims=True)
        p1.append(e1 / jnp.where(z1 > 0, z1, 1.0))
    o_cmp = [jnp.dot(vct_ref[g * d:(g + 1) * d, :].astype(BF16), p1[g].astype(BF16),
                     preferred_element_type=F32) for g in groups]

    nblk = seq // NSA_SEL_LEN
    blk = lax.broadcasted_iota(jnp.int32, (LANES, TQ), 0)
    cur = (t0 + lax.broadcasted_iota(jnp.int32, (LANES, TQ), 1)) >> SEL_SHIFT
    forced = (blk == 0) | (blk == cur) | (blk == cur - 1)
    imp = []
    for g in groups:
        p1sum = p1[g][:, 0:TQ]
        for r in range(1, R):
            p1sum = p1sum + p1[g][:, r * TQ:(r + 1) * TQ]
        v = _dot_exact01(ovt_ref[...], p1sum)
        v = jnp.where(blk > cur, SEL_FUTURE, v)
        v = jnp.where(forced, SEL_FORCED, v)
        imp.append(jnp.where(blk >= nblk, SEL_TAKEN, v))
    sel = [jnp.zeros((LANES, TQ), F32) for _ in groups]
    for _ in range(min(NSA_N_SEL, nblk)):
        for g in groups:
            mx = jnp.max(imp[g], axis=0, keepdims=True)
            first = jnp.min(jnp.where(imp[g] == mx, blk, LANES), axis=0, keepdims=True)
            hit = blk == first
            sel[g] = jnp.where(hit, 1.0, sel[g])
            imp[g] = jnp.where(hit, SEL_TAKEN, imp[g])
    q_aug = []
    for g in groups:
        bias = jnp.where(sel[g] > 0, 0.0, SEL_MASK).T.astype(BF16)
        q_aug.append(jnp.concatenate([qs2[g], jnp.concatenate([bias] * R, axis=0)], axis=1))

    krow = lax.broadcasted_iota(jnp.int32, (TK, NQ), 0)
    acc_ref[...] = jnp.zeros_like(acc_ref)

    def sel_tile(j, ml, masked):
        s = [lax.dot_general(kaug_ref[g, j], q_aug[g], NT_DIMS, preferred_element_type=F32)
             for g in groups]
        out = []
        for g in groups:
            m_old, l_old = ml[2 * g], ml[2 * g + 1]
            sg = jnp.where(j * TK + krow <= tcol, s[g], -jnp.inf) if masked else s[g]
            m_new = jnp.maximum(m_old, jnp.max(sg, axis=0, keepdims=True))
            alpha = jnp.exp2(m_old - m_new)
            p = jnp.exp2(sg - m_new)
            l_new = alpha * l_old + jnp.sum(p, axis=0, keepdims=True)
            acc_ref[g] = alpha * acc_ref[g] + jnp.dot(vst_ref[g, j], p.astype(BF16),
                                                      preferred_element_type=F32)
            out += [m_new, l_new]
        return tuple(out)

    n_tiles = (t0 + TQ + TK - 1) // TK
    ml0 = (jnp.full((1, NQ), -jnp.inf, F32), jnp.zeros((1, NQ), F32)) * G
    ml = lax.fori_loop(0, n_tiles - 1, lambda j, c: sel_tile(j, c, False), ml0)
    ml = sel_tile(n_tiles - 1, ml, True)
    o_slc = [acc_ref[g] / ml[2 * g + 1] for g in groups]

    W = NSA_WINDOW
    span = min(W + TQ, seq)
    kstart = pl.multiple_of(jnp.maximum(t0 + TQ - span, 0), TQ)
    kb = kstart // TQ
    rel = tcol - kstart - lax.broadcasted_iota(jnp.int32, (span, NQ), 0)
    win_ok = (rel >= 0) & (rel < W)
    s3 = [lax.dot_general(kwb_ref[g, pl.ds(kstart, span), :], qs2[g], NT_DIMS,
                          preferred_element_type=F32) for g in groups]
    o_win = []
    for g in groups:
        s = jnp.where(win_ok, s3[g], -jnp.inf)
        e3 = jnp.exp2(s - jnp.max(s, axis=0, keepdims=True))
        z3 = jnp.sum(e3, axis=0, keepdims=True)
        vwt = jnp.concatenate([vwt_ref[g, kb + h] for h in range(span // TQ)], axis=1)
        o_win.append(jnp.dot(vwt, e3.astype(BF16), preferred_element_type=F32) / z3)

    gt = _sigmoid(gt_ref[...]).T
    for g in groups:
        for r in range(R):
            c = 3 * (g * R + r)
            cols = slice(r * TQ, (r + 1) * TQ)
            o = (gt[c:c + 1] * o_cmp[g][:, cols] + gt[c + 1:c + 2] * o_slc[g][:, cols]
                 + gt[c + 2:c + 3] * o_win[g][:, cols])
            o_ref[:, (g * R + r) * d:(g * R + r + 1) * d] = o.T.astype(o_ref.dtype)


def nsa_attention(y3, k_cmp, v_cmp):
    batch, seq, n = y3.shape
    m = batch * seq
    TQ, TK, d = NSA_TQ, NSA_TK, NSA_DH
    R = NSA_HEADS // NSA_KV
    nq = seq // TQ
    ncp = k_cmp.shape[1]
    assert seq // NSA_SEL_LEN <= LANES and seq % TK == 0
    y2 = y3.reshape(m, n)
    v_cmp_t = v_cmp.transpose(0, 2, 1)
    nc = (seq - NSA_CMP_LEN) // NSA_CMP_STRIDE + 1
    cs = jnp.arange(ncp) * NSA_CMP_STRIDE
    ss = jnp.arange(LANES) * NSA_SEL_LEN
    overlap_t = ((cs[None, :] < ss[:, None] + NSA_SEL_LEN) & (cs[None, :] + NSA_CMP_LEN > ss[:, None])
                 & (jnp.arange(ncp)[None, :] < nc) & (jnp.arange(LANES)[:, None] < seq // NSA_SEL_LEN))
    overlap_t = overlap_t.astype(BF16)
    G = NSA_KV
    nk = seq // TK
    kaug, vst, kwb, vwt = nsa_prep(y3)
    whole = lambda shape: pl.BlockSpec((None,) + shape, lambda b, i: (b,) + (0,) * len(shape))
    return pl.pallas_call(
        functools.partial(_nsa_attn_kernel, seq=seq),
        out_shape=jax.ShapeDtypeStruct((m, MIX_HALF), BF16),
        grid=(batch, nq),
        in_specs=[
            pl.BlockSpec((TQ, NSA_HEADS * d), lambda b, i: (b * nq + i, 0)),
            pl.BlockSpec((TQ, LANES), lambda b, i: (b * nq + i, CD_GATE_COL // LANES)),
            whole((ncp, G * d)), whole((G * d, ncp)),
            whole((G, nk, TK, d + LANES)), whole((G, nk, d, TK)),
            whole((G, seq, d)), whole((G, seq // TQ, d, TQ)),
            pl.BlockSpec((LANES, ncp), lambda b, i: (0, 0)),
        ],
        out_specs=pl.BlockSpec((TQ, NSA_HEADS * d), lambda b, i: (b * nq + i, 0)),
        scratch_shapes=[pltpu.VMEM((G, d, R * TQ), F32)],
        compiler_params=_params(("parallel", "arbitrary")),
        name="nsa_attention",
    )(y2, y2, k_cmp, v_cmp_t, kaug, vst, kwb, vwt, overlap_t)


def _router_kernel(x_ref, nw_ref, w2_ref, wh_ref, o_ref):
    x = x_ref[...]
    xn = x * lax.rsqrt(jnp.mean(x * x, axis=-1, keepdims=True) + RMS_EPS) * nw_ref[...]
    xh = xn.astype(BF16)
    xl = (xn - xh.astype(F32)).astype(BF16)
    r1 = jnp.dot(xh, w2_ref[...], preferred_element_type=F32)
    r2 = jnp.dot(xl, wh_ref[...], preferred_element_type=F32)
    o_ref[...] = r1[:, 0:LANES] + r1[:, LANES:] + r2


def router_logits(h, nw, w_grp, w_exp, tm):
    m, d = h.shape
    wr = jnp.zeros((d, LANES), F32).at[:, 0:MOE_GROUPS].set(w_grp)
    wr = wr.at[:, MOE_GROUPS:MOE_GROUPS + MOE_EXPERTS].set(w_exp)
    wh = wr.astype(BF16)
    wl = (wr - wh.astype(F32)).astype(BF16)
    return pl.pallas_call(
        _router_kernel,
        out_shape=jax.ShapeDtypeStruct((m, LANES), F32),
        grid=(m // tm,),
        in_specs=[
            pl.BlockSpec((tm, d), lambda i: (i, 0)),
            pl.BlockSpec((1, d), lambda i: (0, 0)),
            pl.BlockSpec((d, 2 * LANES), lambda i: (0, 0)),
            pl.BlockSpec((d, LANES), lambda i: (0, 0)),
        ],
        out_specs=pl.BlockSpec((tm, LANES), lambda i: (i, 0)),
        compiler_params=_params(("parallel",)),
        name="router_logits",
    )(h, nw.reshape(1, d), jnp.concatenate([wh, wl], axis=1), wh)


def _cast3_kernel(a_ref, b_ref, c_ref, oa_ref, ob_ref, oc_ref):
    oa_ref[...] = a_ref[...].astype(oa_ref.dtype)
    ob_ref[...] = b_ref[...].astype(ob_ref.dtype)
    oc_ref[...] = c_ref[...].astype(oc_ref.dtype)


def expert_weights_bf16(w_gate, w_up, w_down, layer):
    _, E, d, dff = w_gate.shape
    split = 2
    in_spec = lambda r, c: pl.BlockSpec((None, None, r // split, c), lambda e, s: (layer, e, s, 0))
    out_spec = lambda r, c: pl.BlockSpec((None, r // split, c), lambda e, s: (e, s, 0))
    return pl.pallas_call(
        _cast3_kernel,
        out_shape=(jax.ShapeDtypeStruct((E, d, dff), BF16), jax.ShapeDtypeStruct((E, d, dff), BF16),
                   jax.ShapeDtypeStruct((E, dff, d), BF16)),
        grid=(E, split),
        in_specs=[in_spec(d, dff), in_spec(d, dff), in_spec(dff, d)],
        out_specs=(out_spec(d, dff), out_spec(d, dff), out_spec(dff, d)),
        compiler_params=_params(("parallel", "parallel")),
        name="expert_weights_bf16",
    )(w_gate, w_up, w_down)


def _gather_rows(idx_ref, base, src_ref, dst_ref, sem, n_rows):
    def body(r, carry):
        tok = idx_ref[base + r]
        pltpu.make_async_copy(src_ref.at[pl.ds(tok, 1)], dst_ref.at[pl.ds(r, 1)], sem).start()
        return carry
    lax.fori_loop(0, n_rows, body, 0, unroll=8)


def _wait_rows(src_ref, dst_ref, sem, n_rows):
    pltpu.make_async_copy(src_ref.at[pl.ds(0, n_rows)], dst_ref, sem).wait()


def _expert_kernel(be_ref, nused_ref, tok_ref, h_ref, nw_ref, wg_ref, wu_ref, wd_ref, o_ref,
                   xbuf, sems):
    TM = MOE_TM
    i = pl.program_id(0)
    n_used = nused_ref[0]
    slot = i % 2

    @pl.when(i == 0)
    def _():
        _gather_rows(tok_ref, 0, h_ref, xbuf.at[0], sems.at[0], TM)

    @pl.when(i + 1 < n_used)
    def _():
        _gather_rows(tok_ref, (i + 1) * TM, h_ref, xbuf.at[1 - slot], sems.at[1 - slot], TM)

    @pl.when(i < n_used)
    def _():
        _wait_rows(h_ref, xbuf.at[slot], sems.at[slot], TM)
        x = xbuf[slot]
        xn = (x * lax.rsqrt(jnp.mean(x * x, axis=-1, keepdims=True) + RMS_EPS) * nw_ref[...]).astype(BF16)
        hg = jnp.dot(xn, wg_ref[...], preferred_element_type=F32)
        hu = jnp.dot(xn, wu_ref[...], preferred_element_type=F32)
        hb = (_silu(hg) * hu).astype(BF16)
        o_ref[...] = jnp.dot(hb, wd_ref[...], preferred_element_type=F32)

    @pl.when(i >= n_used)
    def _():
        o_ref[...] = jnp.zeros_like(o_ref)


def expert_ffn(h, nw, row_tok, blk_expert, n_used, w_gate, w_up, w_down):
    m, d = h.shape
    rows = row_tok.shape[0]
    TM = MOE_TM
    nb = rows // TM
    dff = w_gate.shape[2]
    grid_spec = pltpu.PrefetchScalarGridSpec(
        num_scalar_prefetch=3,
        grid=(nb,),
        in_specs=[
            pl.BlockSpec(memory_space=pl.ANY),
            pl.BlockSpec((1, d), lambda i, be, nu, tok: (0, 0)),
            pl.BlockSpec((None, d, dff), lambda i, be, nu, tok: (be[i], 0, 0)),
            pl.BlockSpec((None, d, dff), lambda i, be, nu, tok: (be[i], 0, 0)),
            pl.BlockSpec((None, dff, d), lambda i, be, nu, tok: (be[i], 0, 0)),
        ],
        out_specs=pl.BlockSpec((TM, d), lambda i, be, nu, tok: (i, 0)),
        scratch_shapes=[pltpu.VMEM((2, TM, d), F32), pltpu.SemaphoreType.DMA((2,))],
    )
    return pl.pallas_call(
        _expert_kernel,
        out_shape=jax.ShapeDtypeStruct((rows, d), F32),
        grid_spec=grid_spec,
        compiler_params=_params(("arbitrary",)),
        name="expert_ffn",
    )(blk_expert, n_used, row_tok, h, nw.reshape(1, d), w_gate, w_up, w_down)


def _combine_kernel(dest_ref, ys_ref, h_ref, gate_ref, fw_ref, o_ref, ybuf, sems, *, final_norm):
    tc = h_ref.shape[0]
    i = pl.program_id(0)
    n = pl.num_programs(0)
    slot = i % 2

    @pl.when(i == 0)
    def _():
        _gather_rows(dest_ref, 0, ys_ref, ybuf.at[0], sems.at[0], 2 * tc)

    @pl.when(i + 1 < n)
    def _():
        _gather_rows(dest_ref, (i + 1) * 2 * tc, ys_ref, ybuf.at[1 - slot], sems.at[1 - slot], 2 * tc)

    _wait_rows(ys_ref, ybuf.at[slot], sems.at[slot], 2 * tc)
    gate = gate_ref[...]
    out = h_ref[...] + (ybuf[slot, 0:tc, :] * gate[:, 0:1] + ybuf[slot, tc:, :] * gate[:, 1:2])
    if final_norm:
        out = out * lax.rsqrt(jnp.mean(out * out, axis=-1, keepdims=True) + RMS_EPS) * fw_ref[...]
    o_ref[...] = out


def moe_combine(h, ys, dest_tiles, gate, final_w, tc, final_norm):
    m, d = h.shape
    grid_spec = pltpu.PrefetchScalarGridSpec(
        num_scalar_prefetch=1,
        grid=(m // tc,),
        in_specs=[
            pl.BlockSpec(memory_space=pl.ANY),
            pl.BlockSpec((tc, d), lambda i, dst: (i, 0)),
            pl.BlockSpec((tc, MOE_TOPK), lambda i, dst: (i, 0)),
            pl.BlockSpec((1, d), lambda i, dst: (0, 0)),
        ],
        out_specs=pl.BlockSpec((tc, d), lambda i, dst: (i, 0)),
        scratch_shapes=[pltpu.VMEM((2, 2 * tc, d), F32), pltpu.SemaphoreType.DMA((2,))],
    )
    return pl.pallas_call(
        functools.partial(_combine_kernel, final_norm=final_norm),
        out_shape=jax.ShapeDtypeStruct((m, d), F32),
        grid_spec=grid_spec,
        compiler_params=_params(("arbitrary",)),
        name="moe_combine",
    )(dest_tiles, ys, h, gate, final_w.reshape(1, d))


def _route(logits, b_grp, b_exp):
    n = logits.shape[0]
    K, E, TM = MOE_TOPK, MOE_EXPERTS, MOE_TM
    grp_logits = logits[:, 0:MOE_GROUPS] + b_grp.astype(F32)
    grp_prob = jax.nn.softmax(grp_logits, axis=-1)
    g_idx = jnp.argmax(grp_logits, axis=-1)
    p_grp = jnp.take_along_axis(grp_prob, g_idx[:, None], axis=1)[:, 0]
    exp_logits = (logits[:, MOE_GROUPS:MOE_GROUPS + E] + b_exp.astype(F32)).reshape(n, MOE_GROUPS, MOE_EPG)
    within = jnp.take_along_axis(exp_logits, g_idx[:, None, None], axis=1)[:, 0]
    top_p, top_i = lax.top_k(jax.nn.softmax(within, axis=-1), K)
    gate = p_grp[:, None] * top_p / jnp.sum(top_p, axis=-1, keepdims=True)
    expert = (g_idx[:, None] * MOE_EPG + top_i).astype(jnp.int32)

    A = n * K
    flat_e = expert.reshape(A)
    iota = jnp.arange(A, dtype=jnp.int32)
    e_sorted, order = lax.sort((flat_e, iota), num_keys=1)
    counts = jnp.sum((flat_e[:, None] == jnp.arange(E, dtype=jnp.int32)[None, :]).astype(jnp.int32), axis=0)
    padded = (counts + TM - 1) // TM * TM
    pad_end = jnp.cumsum(padded)
    pad_start = pad_end - padded
    start = jnp.cumsum(counts) - counts
    dest_sorted = (pad_start[e_sorted] + iota - start[e_sorted]).astype(jnp.int32)
    _, dest = lax.sort((order, dest_sorted), num_keys=1)
    rows = A + E * TM
    nb = rows // TM
    blk_expert = jnp.minimum(jnp.sum((jnp.arange(nb, dtype=jnp.int32)[:, None] * TM >= pad_end[None, :])
                                     .astype(jnp.int32), axis=1), E - 1).astype(jnp.int32)
    r = jnp.arange(rows, dtype=jnp.int32)
    e_row = jnp.repeat(blk_expert, TM)
    pos = r - pad_start[e_row].astype(jnp.int32)
    src = jnp.clip(start[e_row].astype(jnp.int32) + pos, 0, A - 1)
    row_tok = jnp.where(pos < counts[e_row], order[src] // K, 0).astype(jnp.int32)
    n_used = (pad_end[-1] // TM).astype(jnp.int32).reshape(1)
    return gate.astype(F32), dest.reshape(n, K), row_tok, blk_expert, n_used


def hier_moe(h, nw, w_grp, b_grp, w_exp, b_exp, w_gate_all, w_up_all, w_down_all, layer, final_w,
             final_norm):
    m, d = h.shape
    tc = 256
    logits = router_logits(h, nw, w_grp, w_exp, 512)
    gate, dest, row_tok, blk_expert, n_used = _route(logits, b_grp, b_exp)
    w_gate, w_up, w_down = expert_weights_bf16(w_gate_all, w_up_all, w_down_all, layer)
    ys = expert_ffn(h, nw, row_tok, blk_expert, n_used, w_gate, w_up, w_down)
    dest_tiles = dest.reshape(m // tc, tc, MOE_TOPK).transpose(0, 2, 1).reshape(-1)
    return moe_combine(h, ys, dest_tiles, gate, final_w, tc, final_norm)


def _cd_in_weights(w):
    gates = w[:, 2560:2584]
    pad = jnp.zeros((w.shape[0], CD_N - CD_GATE_COL - gates.shape[1]), w.dtype)
    return jnp.concatenate([w[:, 0:2560], w[:, 2584:], gates, pad], axis=1)


def mixer_ab(h, nw, w_in, w_out, lb, hg_norm, sw_sinks, batch, seq):
    y = norm_matmul(h, nw, w_in.astype(BF16), 512, 1408)
    o_a = hgrn2(y, lb, hg_norm, batch, seq, min(512, seq))
    o_b = swa(y, sw_sinks, batch, seq)
    return outproj_residual(o_a, o_b, w_out.astype(BF16), h, 512)


def mixer_cd(h, nw, w_in, w_out, pe_k, w1_k, w2_k, pe_v, w1_v, w2_v, ret_norm, batch, seq):
    y = norm_matmul(h, nw, _cd_in_weights(w_in).astype(BF16), 512, 1920)
    y3 = y.reshape(batch, seq, CD_N)
    k_cmp, v_cmp = nsa_compress(y3, pe_k, w1_k, w2_k, pe_v, w1_v, w2_v)
    o_c = nsa_attention(y3, k_cmp, v_cmp)
    o_d = retention(y, ret_norm, batch, seq, min(512, seq))
    return outproj_residual(o_c, o_d, w_out.astype(BF16), h, 512)


def kernel(x, norm_mix, norm_ffn, norm_final, ab_w_in, ab_w_out, hg_lb_logits, hg_norm, sw_sinks, cd_w_in, cd_w_out, nsa_pe_k, nsa_w1_k, nsa_w2_k, nsa_pe_v, nsa_w1_v, nsa_w2_v, ret_norm, moe_w_grp, moe_b_grp, moe_w_exp, moe_b_exp, moe_w_gate, moe_w_up, moe_w_down):
    batch, seq, d = x.shape
    depth = norm_mix.shape[0]
    lb_all = jnp.cumsum(jax.nn.softmax(hg_lb_logits.astype(F32), axis=0), axis=0)
    h = x.reshape(batch * seq, d)
    for layer in range(depth):
        if layer % 2 == 0:
            e = layer // 2
            h = mixer_ab(h, norm_mix[layer], ab_w_in[e], ab_w_out[e], lb_all[layer], hg_norm[e],
                         sw_sinks[e], batch, seq)
        else:
            o = layer // 2
            h = mixer_cd(h, norm_mix[layer], cd_w_in[o], cd_w_out[o], nsa_pe_k[o], nsa_w1_k[o],
                         nsa_w2_k[o], nsa_pe_v[o], nsa_w1_v[o], nsa_w2_v[o], ret_norm[o], batch, seq)
        h = hier_moe(h, norm_ffn[layer], moe_w_grp[layer], moe_b_grp[layer], moe_w_exp[layer],
                     moe_b_exp[layer], moe_w_gate, moe_w_up, moe_w_down, layer,
                     norm_final, layer == depth - 1)
    return h.reshape(batch, seq, d)
```

```python
import functools

import jax
import jax.numpy as jnp
from jax import lax
from jax.experimental import pallas as pl
from jax.experimental.pallas import tpu as pltpu

F32 = jnp.float32
BF16 = jnp.bfloat16

D_MODEL = 2048
MIX_HALF = D_MODEL // 2
RMS_EPS = 1e-6
LANES = 128
VMEM_LIMIT = 56 * 1024 * 1024

HG_DK = 128
HG_HEADS = 8
HG_CHUNK = 64
HG_SUB = 16
SW_DH = 128
SW_HEADS = 8
SW_KV = 2
SW_WINDOW = 128
NSA_DH = 128
NSA_HEADS = 8
NSA_KV = 2
NSA_CMP_LEN = 32
NSA_CMP_STRIDE = 16
NSA_CMP_HIDDEN = 128
NSA_SEL_LEN = 64
NSA_N_SEL = 8
NSA_WINDOW = 512
RET_DK = 128
RET_DV = 256
RET_HEADS = 4
RET_CHUNK = 128
RET_THETA_BASE = 10000.0
MOE_GROUPS = 4
MOE_EPG = 8
MOE_EXPERTS = 32
MOE_TOPK = 2
MOE_DFF = 1024
MOE_TM = 256

AB_N = 5632
CD_N = 5760
CD_GATE_COL = 5632

NT_DIMS = (((1,), (1,)), ((), ()))
TN_DIMS = (((0,), (0,)), ((), ()))


def _params(sem, vmem=VMEM_LIMIT):
    return pltpu.CompilerParams(dimension_semantics=sem, vmem_limit_bytes=vmem)


def _sigmoid(x):
    return 1.0 / (1.0 + jnp.exp(-x))


def _silu(x):
    return x * _sigmoid(x)


def _dot_exact01(a01, x):
    n = x.shape[1]
    hi = x.astype(BF16)
    rem = x - hi.astype(F32)
    mid = rem.astype(BF16)
    lo = (rem - mid.astype(F32)).astype(BF16)
    r = jnp.dot(a01, jnp.concatenate([hi, mid, lo], axis=1), preferred_element_type=F32)
    return r[:, 0:n] + r[:, n:2 * n] + r[:, 2 * n:]


def _norm_matmul_kernel(x_ref, nw_ref, w_ref, o_ref, xn_ref):
    @pl.when(pl.program_id(1) == 0)
    def _():
        x = x_ref[...]
        ms = jnp.mean(x * x, axis=-1, keepdims=True)
        xn_ref[...] = (x * lax.rsqrt(ms + RMS_EPS) * nw_ref[...]).astype(BF16)

    o_ref[...] = jnp.dot(xn_ref[...], w_ref[...], preferred_element_type=F32)


def norm_matmul(x, nw, w, tm, tn):
    m, d = x.shape
    n = w.shape[1]
    return pl.pallas_call(
        _norm_matmul_kernel,
        out_shape=jax.ShapeDtypeStruct((m, n), F32),
        grid=(m // tm, n // tn),
        in_specs=[
            pl.BlockSpec((tm, d), lambda i, j: (i, 0)),
            pl.BlockSpec((1, d), lambda i, j: (0, 0)),
            pl.BlockSpec((d, tn), lambda i, j: (0, j)),
        ],
        out_specs=pl.BlockSpec((tm, tn), lambda i, j: (i, j)),
        scratch_shapes=[pltpu.VMEM((tm, d), BF16)],
        compiler_params=_params(("parallel", "arbitrary")),
        name="norm_matmul",
    )(x, nw.reshape(1, d), w)


def _outproj_kernel(a_ref, b_ref, w_ref, h_ref, o_ref):
    half = a_ref.shape[1]
    acc = jnp.dot(a_ref[...], w_ref[0:half, :], preferred_element_type=F32)
    acc = acc + jnp.dot(b_ref[...], w_ref[half:, :], preferred_element_type=F32)
    o_ref[...] = h_ref[...] + acc


def outproj_residual(a, b, w, h, tm):
    m, half = a.shape
    d = w.shape[1]
    return pl.pallas_call(
        _outproj_kernel,
        out_shape=jax.ShapeDtypeStruct((m, d), F32),
        grid=(m // tm,),
        in_specs=[
            pl.BlockSpec((tm, half), lambda i: (i, 0)),
            pl.BlockSpec((tm, half), lambda i: (i, 0)),
            pl.BlockSpec((2 * half, d), lambda i: (0, 0)),
            pl.BlockSpec((tm, d), lambda i: (i, 0)),
        ],
        out_specs=pl.BlockSpec((tm, d), lambda i: (i, 0)),
        compiler_params=_params(("parallel",)),
        name="outproj_residual",
    )(a, b, w, h)


def _hgrn2_kernel(q_ref, f_ref, i_ref, g_ref, lb_ref, nw_ref, o_ref, st_ref, *, n_chunks):
    C, c = HG_CHUNK, HG_SUB
    nsub = C // c

    @pl.when(pl.program_id(2) == 0)
    def _():
        st_ref[...] = jnp.zeros_like(st_ref)

    lb = lb_ref[...]
    nw = nw_ref[...]
    tri = (lax.broadcasted_iota(jnp.int32, (C, C), 0)
           >= lax.broadcasted_iota(jnp.int32, (C, C), 1)).astype(BF16)
    hs = c // 2
    row8 = lax.broadcasted_iota(jnp.int32, (1, hs, 1), 1)
    chunks = range(n_chunks)
    rows_of = lambda ci: slice(ci * C, (ci + 1) * C)

    q = [q_ref[rows_of(ci), :] for ci in chunks]
    v = [i_ref[rows_of(ci), :] for ci in chunks]
    vb = [x.astype(BF16) for x in v]
    f = [lb + (1.0 - lb) * _sigmoid(f_ref[rows_of(ci), :]) for ci in chunks]
    k = [1.0 - x for x in f]
    b = [_dot_exact01(tri, jnp.log(x)) for x in f]

    a_off = []
    for ci in chunks:
        for i in range(1, nsub):
            r = b[ci][i * c - 1:i * c, :]
            qi = q[ci][i * c:(i + 1) * c] * jnp.exp(b[ci][i * c:(i + 1) * c] - r)
            ki = k[ci][0:i * c] * jnp.exp(r - b[ci][0:i * c])
            a_off.append(lax.dot_general(qi.astype(BF16), ki.astype(BF16), NT_DIMS,
                                         preferred_element_type=F32))
    o_off = []
    for ci in chunks:
        rows = [jnp.zeros((c, HG_DK), F32)]
        for i in range(1, nsub):
            a = a_off[ci * (nsub - 1) + i - 1]
            rows.append(jnp.dot(a.astype(BF16), vb[ci][0:i * c], preferred_element_type=F32))
        o_off.append(jnp.concatenate(rows, axis=0))
    inc = []
    for ci in chunks:
        bend = b[ci][C - 1:C, :]
        kd = k[ci] * jnp.exp(bend - b[ci])
        inc.append(lax.dot_general(vb[ci], kd.astype(BF16), TN_DIMS, preferred_element_type=F32))
    st = st_ref[...]
    o_st = []
    for ci in chunks:
        o_st.append(lax.dot_general((q[ci] * jnp.exp(b[ci])).astype(BF16), st.astype(BF16), NT_DIMS,
                                    preferred_element_type=F32))
        st = st * jnp.exp(b[ci][C - 1:C, :]) + inc[ci]
    st_ref[...] = st

    for ci in chunks:
        b4 = b[ci].reshape(nsub, c, HG_DK)
        q4 = q[ci].reshape(nsub, c, HG_DK)
        k4 = k[ci].reshape(nsub, c, HG_DK)
        v4 = v[ci].reshape(nsub, c, HG_DK)
        bt, bb = b4[:, 0:hs], b4[:, hs:]
        qt, qb = q4[:, 0:hs], q4[:, hs:]
        od_t = jnp.zeros((nsub, hs, HG_DK), F32)
        od_b = jnp.zeros((nsub, hs, HG_DK), F32)
        for s in range(c):
            piv, kp, vp = b4[:, s:s + 1], k4[:, s:s + 1], v4[:, s:s + 1]
            if s < hs:
                e = jnp.exp(jnp.minimum(bt - piv, 0.0))
                a = jnp.sum(qt * e * kp, axis=-1, keepdims=True)
                od_t = od_t + jnp.where(row8 >= s, a, 0.0) * vp
                e = jnp.exp(jnp.minimum(bb - piv, 0.0))
                od_b = od_b + jnp.sum(qb * e * kp, axis=-1, keepdims=True) * vp
            else:
                e = jnp.exp(jnp.minimum(bb - piv, 0.0))
                a = jnp.sum(qb * e * kp, axis=-1, keepdims=True)
                od_b = od_b + jnp.where(row8 >= s - hs, a, 0.0) * vp
        o = o_off[ci] + o_st[ci] + jnp.concatenate([od_t, od_b], axis=1).reshape(C, HG_DK)
        y = o * lax.rsqrt(jnp.mean(o * o, axis=-1, keepdims=True) + RMS_EPS) * nw
        o_ref[rows_of(ci), :] = (y * _silu(g_ref[rows_of(ci), :])).astype(o_ref.dtype)


def hgrn2(y, lb, nw, batch, seq, tc):
    m = y.shape[0]
    nt = seq // tc
    row = lambda b, h, t: b * nt + t
    hd = HG_HEADS
    return pl.pallas_call(
        functools.partial(_hgrn2_kernel, n_chunks=tc // HG_CHUNK),
        out_shape=jax.ShapeDtypeStruct((m, MIX_HALF), BF16),
        grid=(batch, hd, nt),
        in_specs=[
            pl.BlockSpec((tc, HG_DK), lambda b, h, t: (row(b, h, t), h)),
            pl.BlockSpec((tc, HG_DK), lambda b, h, t: (row(b, h, t), hd + h)),
            pl.BlockSpec((tc, HG_DK), lambda b, h, t: (row(b, h, t), 2 * hd + h)),
            pl.BlockSpec((tc, HG_DK), lambda b, h, t: (row(b, h, t), 3 * hd + h)),
            pl.BlockSpec((None, 1, HG_DK), lambda b, h, t: (h, 0, 0)),
            pl.BlockSpec((None, 1, HG_DK), lambda b, h, t: (h, 0, 0)),
        ],
        out_specs=pl.BlockSpec((tc, HG_DK), lambda b, h, t: (row(b, h, t), h)),
        scratch_shapes=[pltpu.VMEM((HG_DK, HG_DK), F32)],
        compiler_params=_params(("parallel", "parallel", "arbitrary")),
        name="hgrn2",
    )(y, y, y, y, lb.reshape(hd, 1, HG_DK), nw.reshape(hd, 1, HG_DK))


def _swa_kernel(q_ref, kp_ref, kc_ref, vp_ref, vc_ref, sink_ref, o_ref):
    W, d = SW_WINDOW, SW_DH
    G = SW_KV
    R = SW_HEADS // G
    NQ = R * W
    groups = range(G)
    n = pl.program_id(1)
    q = q_ref[...]
    qs = [(jnp.concatenate([q[:, (g * R + r) * d:(g * R + r + 1) * d] for r in range(R)], axis=0)
           * (d ** -0.5 * LOG2E)).astype(BF16) for g in groups]
    kk = [jnp.concatenate([kp_ref[:, g * d:(g + 1) * d], kc_ref[:, g * d:(g + 1) * d]],
                          axis=0).astype(BF16) for g in groups]
    vt = [jnp.concatenate([vp_ref[:, g * d:(g + 1) * d].T, vc_ref[:, g * d:(g + 1) * d].T],
                          axis=1).astype(BF16) for g in groups]
    s = [lax.dot_general(kk[g], qs[g], NT_DIMS, preferred_element_type=F32) for g in groups]
    kj = lax.broadcasted_iota(jnp.int32, (2 * W, NQ), 0)
    rel = (lax.broadcasted_iota(jnp.int32, (2 * W, NQ), 1) & (W - 1)) + W - kj
    valid = (rel >= 0) & (rel < W) & ((kj >= W) | (n > 0))
    outs = []
    for g in groups:
        sg = jnp.where(valid, s[g], -jnp.inf)
        sink = sink_ref[g] * LOG2E
        mx = jnp.maximum(jnp.max(sg, axis=0, keepdims=True), sink)
        p = jnp.exp2(sg - mx)
        z = jnp.sum(p, axis=0, keepdims=True) + jnp.exp2(sink - mx)
        outs.append(jnp.dot(vt[g], p.astype(BF16), preferred_element_type=F32) / z)
    for g in groups:
        for r in range(R):
            o_ref[:, (g * R + r) * d:(g * R + r + 1) * d] = outs[g][:, r * W:(r + 1) * W].T.astype(o_ref.dtype)


def swa(y, sinks, batch, seq):
    m = y.shape[0]
    W, d, G = SW_WINDOW, SW_DH, SW_KV
    R = SW_HEADS // G
    nb = seq // W
    qcol = 4 * MIX_HALF // (SW_HEADS * d)
    kcol = (4 * MIX_HALF + SW_HEADS * d) // (G * d)
    vcol = kcol + 1
    cur = lambda b, n: b * nb + n
    prev = lambda b, n: b * nb + jnp.maximum(n - 1, 0)
    sink_cols = jnp.repeat(sinks.astype(F32).reshape(G, R), W, axis=1).reshape(G, 1, R * W)
    return pl.pallas_call(
        _swa_kernel,
        out_shape=jax.ShapeDtypeStruct((m, MIX_HALF), BF16),
        grid=(batch, nb),
        in_specs=[
            pl.BlockSpec((W, SW_HEADS * d), lambda b, n: (cur(b, n), qcol)),
            pl.BlockSpec((W, G * d), lambda b, n: (prev(b, n), kcol)),
            pl.BlockSpec((W, G * d), lambda b, n: (cur(b, n), kcol)),
            pl.BlockSpec((W, G * d), lambda b, n: (prev(b, n), vcol)),
            pl.BlockSpec((W, G * d), lambda b, n: (cur(b, n), vcol)),
            pl.BlockSpec((G, 1, R * W), lambda b, n: (0, 0, 0)),
        ],
        out_specs=pl.BlockSpec((W, SW_HEADS * d), lambda b, n: (cur(b, n), 0)),
        compiler_params=_params(("parallel", "parallel")),
        name="swa",
    )(y, y, y, y, y, sink_cols)


def _retention_kernel(q_ref, k_ref, v_ref, g_ref, cos_ref, sin_ref, dmat_ref, xi_ref, zeta_ref,
                      cdec_ref, nw_ref, o_ref, st_ref, *, n_chunks):
    C = RET_CHUNK

    @pl.when(pl.program_id(2) == 0)
    def _():
        st_ref[...] = jnp.zeros_like(st_ref)

    dmat = dmat_ref[...]
    xi = xi_ref[...]
    zeta = zeta_ref[...]
    cdec = cdec_ref[...]
    nw = nw_ref[...]

    chunks = range(n_chunks)
    rows_of = lambda ci: slice(ci * C, (ci + 1) * C)

    def rotate(x_ref, ci):
        x = x_ref[rows_of(ci), :]
        return x * cos_ref[rows_of(ci), :] + pltpu.roll(x, RET_DK // 2, 1) * sin_ref[rows_of(ci), :]

    qr = [rotate(q_ref, ci) for ci in chunks]
    kr = [rotate(k_ref, ci) * (RET_DK ** -0.5) for ci in chunks]
    vb = [v_ref[rows_of(ci), :].astype(BF16) for ci in chunks]
    inner = [lax.dot_general(qr[ci].astype(BF16), kr[ci].astype(BF16), NT_DIMS,
                             preferred_element_type=F32) * dmat for ci in chunks]
    inc = [lax.dot_general((kr[ci] * zeta).astype(BF16), vb[ci], TN_DIMS,
                           preferred_element_type=F32) for ci in chunks]
    o_in = [jnp.dot(inner[ci].astype(BF16), vb[ci], preferred_element_type=F32) for ci in chunks]
    st = st_ref[...]
    o_cross = []
    for ci in chunks:
        o_cross.append(jnp.dot((qr[ci] * xi).astype(BF16), st.astype(BF16), preferred_element_type=F32))
        st = cdec * st + inc[ci]
    st_ref[...] = st
    for ci in chunks:
        o = o_in[ci] + o_cross[ci]
        cen = o - jnp.mean(o, axis=-1, keepdims=True)
        y = cen * lax.rsqrt(jnp.mean(cen * cen, axis=-1, keepdims=True) + RMS_EPS) * nw
        o_ref[rows_of(ci), :] = (y * _silu(g_ref[rows_of(ci), :])).astype(o_ref.dtype)


def retention(y, nw, batch, seq, tc):
    m = y.shape[0]
    nt = seq // tc
    H, C = RET_HEADS, RET_CHUNK
    log_g = jnp.log1p(-jnp.exp2(-5.0 - jnp.arange(H, dtype=F32)))
    j = jnp.arange(C, dtype=F32)
    diff = j[:, None] - j[None, :]
    dmat = jnp.where(diff >= 0, jnp.exp(jnp.maximum(diff, 0.0)[None] * log_g[:, None, None]), 0.0)
    zeta = jnp.exp((C - 1 - j)[None, :] * log_g[:, None]).reshape(H, C, 1)
    xi = jnp.exp((j + 1.0)[None, :] * log_g[:, None]).reshape(H, C, 1)
    cdec = jnp.broadcast_to(jnp.exp(C * log_g)[:, None, None], (H, 1, RET_DV))
    half = RET_DK // 2
    theta = 1.0 / (RET_THETA_BASE ** jnp.linspace(0.0, 1.0, half, dtype=F32))
    ang = jnp.arange(seq).astype(F32)[:, None] * theta[None, :]
    cos2 = jnp.concatenate([jnp.cos(ang), jnp.cos(ang)], axis=-1)
    sin2 = jnp.concatenate([-jnp.sin(ang), jnp.sin(ang)], axis=-1)

    row = lambda b, h, t: b * nt + t
    qcol = 2560 // RET_DK
    kcol = 3072 // RET_DK
    vcol = 3584 // RET_DV
    gcol = 4608 // RET_DV
    return pl.pallas_call(
        functools.partial(_retention_kernel, n_chunks=tc // C),
        out_shape=jax.ShapeDtypeStruct((m, MIX_HALF), BF16),
        grid=(batch, H, nt),
        in_specs=[
            pl.BlockSpec((tc, RET_DK), lambda b, h, t: (row(b, h, t), qcol + h)),
            pl.BlockSpec((tc, RET_DK), lambda b, h, t: (row(b, h, t), kcol + h)),
            pl.BlockSpec((tc, RET_DV), lambda b, h, t: (row(b, h, t), vcol + h)),
            pl.BlockSpec((tc, RET_DV), lambda b, h, t: (row(b, h, t), gcol + h)),
            pl.BlockSpec((tc, RET_DK), lambda b, h, t: (t, 0)),
            pl.BlockSpec((tc, RET_DK), lambda b, h, t: (t, 0)),
            pl.BlockSpec((None, C, C), lambda b, h, t: (h, 0, 0)),
            pl.BlockSpec((None, C, 1), lambda b, h, t: (h, 0, 0)),
            pl.BlockSpec((None, C, 1), lambda b, h, t: (h, 0, 0)),
            pl.BlockSpec((None, 1, RET_DV), lambda b, h, t: (h, 0, 0)),
            pl.BlockSpec((None, 1, RET_DV), lambda b, h, t: (h, 0, 0)),
        ],
        out_specs=pl.BlockSpec((tc, RET_DV), lambda b, h, t: (row(b, h, t), h)),
        scratch_shapes=[pltpu.VMEM((RET_DK, RET_DV), F32)],
        compiler_params=_params(("parallel", "parallel", "arbitrary")),
        name="retention",
    )(y, y, y, y, cos2, sin2, dmat, xi, zeta, cdec, nw.reshape(H, 1, RET_DV))


def _gelu_tanh(x):
    return 0.5 * x * (1.0 + jnp.tanh(0.7978845608028654 * (x + 0.044715 * x * x * x)))


def _nsa_compress_kernel(xk_ref, xv_ref, pek_ref, pev_ref, w1k_ref, w1v_ref, w2k_ref, w2v_ref,
                         ok_ref, ov_ref):
    gh = NSA_KV * NSA_CMP_HIDDEN

    def one(x_ref, pe_ref, w1_ref, w2_ref, o_ref):
        w1 = w1_ref[...]
        pq = jnp.dot(x_ref[...].astype(BF16), w1, preferred_element_type=F32)
        pe = jnp.dot(pe_ref[...].astype(BF16), w1, preferred_element_type=F32)
        const = pe[0:1, 0:gh] + pe[1:2, gh:]
        nrow = pq.shape[0]
        nxt = pltpu.roll(pq[:, gh:], nrow - 1, 0)
        hid = _gelu_tanh(pq[:, 0:gh] + nxt + const)
        o_ref[...] = jnp.dot(hid.astype(BF16), w2_ref[...], preferred_element_type=F32)

    one(xk_ref, pek_ref, w1k_ref, w2k_ref, ok_ref)
    one(xv_ref, pev_ref, w1v_ref, w2v_ref, ov_ref)


def _compress_weights(pe, w1, w2):
    G, L, S = NSA_KV, NSA_CMP_LEN, NSA_CMP_STRIDE
    d, hd = NSA_DH, NSA_CMP_HIDDEN
    eye = jnp.eye(G, dtype=F32)

    def big(w):
        return jnp.einsum('ldh,ge->lgdeh', w, eye).reshape(S * G * d, G * hd)

    w1big = jnp.concatenate([big(w1[:S]), big(w1[S:])], axis=1).astype(BF16)
    w2big = jnp.einsum('hd,ge->ghed', w2, eye).reshape(G * hd, G * d).astype(BF16)

    def pe_row(p):
        return jnp.broadcast_to(p[:, None, :], (S, G, d)).reshape(S * G * d)

    pe8 = jnp.zeros((8, S * G * d), F32).at[0].set(pe_row(pe[:S])).at[1].set(pe_row(pe[S:]))
    return pe8, w1big, w2big


def nsa_compress(y3, pe_k, w1_k, w2_k, pe_v, w1_v, w2_v):
    batch, seq, _ = y3.shape
    G, S, d = NSA_KV, NSA_CMP_STRIDE, NSA_DH
    nrow = seq // S
    xk = y3[:, :, 1024:1280].reshape(batch, nrow, S * G * d)
    xv = y3[:, :, 1280:1536].reshape(batch, nrow, S * G * d)
    pek, w1k, w2k = _compress_weights(pe_k, w1_k, w2_k)
    pev, w1v, w2v = _compress_weights(pe_v, w1_v, w2_v)
    kdim = S * G * d
    full = lambda shape: pl.BlockSpec(shape, lambda b: (0,) * len(shape))
    out = jax.ShapeDtypeStruct((batch, nrow, G * d), F32)
    return pl.pallas_call(
        _nsa_compress_kernel,
        out_shape=(out, out),
        grid=(batch,),
        in_specs=[
            pl.BlockSpec((None, nrow, kdim), lambda b: (b, 0, 0)),
            pl.BlockSpec((None, nrow, kdim), lambda b: (b, 0, 0)),
            full((8, kdim)), full((8, kdim)),
            full(w1k.shape), full(w1v.shape), full(w2k.shape), full(w2v.shape),
        ],
        out_specs=(pl.BlockSpec((None, nrow, G * d), lambda b: (b, 0, 0)),
                   pl.BlockSpec((None, nrow, G * d), lambda b: (b, 0, 0))),
        compiler_params=_params(("parallel",)),
        name="nsa_compress",
    )(xk, xv, pek, pev, w1k, w1v, w2k, w2v)


NSA_TQ = 128
NSA_TK = 512
SEL_SHIFT = 6
SEL_FORCED = 1e30
SEL_FUTURE = -1e30
SEL_TAKEN = -3e38
SEL_MASK = -1e9
LOG2E = 1.4426950408889634


def _nsa_prep_kernel(ks_ref, vs_ref, kw_ref, vw_ref, kaug_ref, vst_ref, kwb_ref, vwt_ref):
    TK, TQ = NSA_TK, NSA_TQ
    k0 = pl.program_id(2) * TK
    krow = lax.broadcasted_iota(jnp.int32, (TK, LANES), 0)
    bcol = lax.broadcasted_iota(jnp.int32, (TK, LANES), 1)
    onehot = jnp.where(((k0 + krow) >> SEL_SHIFT) == bcol, 1.0, 0.0).astype(BF16)
    kaug_ref[...] = jnp.concatenate([ks_ref[...].astype(BF16), onehot], axis=1)
    vst_ref[...] = vs_ref[...].T.astype(BF16)
    kwb_ref[...] = kw_ref[...].astype(BF16)
    for h in range(TK // TQ):
        vwt_ref[h] = vw_ref[h * TQ:(h + 1) * TQ, :].T.astype(BF16)


def nsa_prep(y3):
    batch, seq, _ = y3.shape
    G, d, TK, TQ = NSA_KV, NSA_DH, NSA_TK, NSA_TQ
    nk = seq // TK
    src = lambda col: pl.BlockSpec((None, TK, d), lambda b, g, c: (b, c, col + g))
    return pl.pallas_call(
        _nsa_prep_kernel,
        out_shape=(jax.ShapeDtypeStruct((batch, G, nk, TK, d + LANES), BF16),
                   jax.ShapeDtypeStruct((batch, G, nk, d, TK), BF16),
                   jax.ShapeDtypeStruct((batch, G, seq, d), BF16),
                   jax.ShapeDtypeStruct((batch, G, seq // TQ, d, TQ), BF16)),
        grid=(batch, G, nk),
        in_specs=[src(1536 // d), src(1792 // d), src(2048 // d), src(2304 // d)],
        out_specs=(pl.BlockSpec((None, None, None, TK, d + LANES), lambda b, g, c: (b, g, c, 0, 0)),
                   pl.BlockSpec((None, None, None, d, TK), lambda b, g, c: (b, g, c, 0, 0)),
                   pl.BlockSpec((None, None, TK, d), lambda b, g, c: (b, g, c, 0)),
                   pl.BlockSpec((None, None, TK // TQ, d, TQ), lambda b, g, c: (b, g, c, 0, 0))),
        compiler_params=_params(("parallel", "parallel", "parallel")),
        name="nsa_prep",
    )(y3, y3, y3, y3)


def _nsa_attn_kernel(q_ref, gt_ref, kc_ref, vct_ref, kaug_ref, vst_ref, kwb_ref, vwt_ref, ovt_ref,
                     o_ref, acc_ref, *, seq):
    TQ, TK, d = NSA_TQ, NSA_TK, NSA_DH
    G = NSA_KV
    R = NSA_HEADS // G
    NQ = R * TQ
    groups = range(G)
    i = pl.program_id(1)
    t0 = i * TQ
    scale = d ** -0.5
    tcol = t0 + (lax.broadcasted_iota(jnp.int32, (1, NQ), 1) & (TQ - 1))

    q = q_ref[...]
    qf = [jnp.concatenate([q[:, (g * R + r) * d:(g * R + r + 1) * d] for r in range(R)], axis=0)
          for g in groups]
    qs = [x.astype(BF16) for x in qf]
    qs2 = [(x * (scale * LOG2E)).astype(BF16) for x in qf]

    ncp = kc_ref.shape[0]
    nid = lax.broadcasted_iota(jnp.int32, (ncp, NQ), 0)
    cmp_ok = nid * NSA_CMP_STRIDE + (NSA_CMP_LEN - 1) <= tcol
    s1 = [lax.dot_general(kc_ref[:, g * d:(g + 1) * d].astype(BF16), qs[g], NT_DIMS,
                          preferred_element_type=F32) * scale for g in groups]
    p1 = []
    for g in groups:
        s = jnp.where(cmp_ok, s1[g], -jnp.inf)
        m1 = jnp.max(s, axis=0, keepdims=True)
        m1 = jnp.where(m1 > -jnp.inf, m1, 0.0)
        e1 = jnp.exp(s - m1)
        z1 = jnp.sum(e1, axis=0, keepdims=True)
        p1.append(e1 / jnp.where(z1 > 0, z1, 1.0))
    o_cmp = [jnp.dot(vct_ref[g * d:(g + 1) * d, :].astype(BF16), p1[g].astype(BF16),
                     preferred_element_type=F32) for g in groups]

    W = NSA_WINDOW
    span = min(W + TQ, seq)
    kstart = pl.multiple_of(jnp.maximum(t0 + TQ - span, 0), TQ)
    kb = kstart // TQ
    rel = tcol - kstart - lax.broadcasted_iota(jnp.int32, (span, NQ), 0)
    win_ok = (rel >= 0) & (rel < W)
    s3 = [lax.dot_general(kwb_ref[g, pl.ds(kstart, span), :], qs2[g], NT_DIMS,
                          preferred_element_type=F32) for g in groups]
    o_win = []
    for g in groups:
        s = jnp.where(win_ok, s3[g], -jnp.inf)
        e3 = jnp.exp2(s - jnp.max(s, axis=0, keepdims=True))
        z3 = jnp.sum(e3, axis=0, keepdims=True)
        vwt = jnp.concatenate([vwt_ref[g, kb + h] for h in range(span // TQ)], axis=1)
        o_win.append(jnp.dot(vwt, e3.astype(BF16), preferred_element_type=F32) / z3)

    nblk = seq // NSA_SEL_LEN
    blk = lax.broadcasted_iota(jnp.int32, (LANES, TQ), 0)
    cur = (t0 + lax.broadcasted_iota(jnp.int32, (LANES, TQ), 1)) >> SEL_SHIFT
    forced = (blk == 0) | (blk == cur) | (blk == cur - 1)
    imp = []
    for g in groups:
        p1sum = p1[g][:, 0:TQ]
        for r in range(1, R):
            p1sum = p1sum + p1[g][:, r * TQ:(r + 1) * TQ]
        v = _dot_exact01(ovt_ref[...], p1sum)
        v = jnp.where(blk > cur, SEL_FUTURE, v)
        v = jnp.where(forced, SEL_FORCED, v)
        imp.append(jnp.where(blk >= nblk, SEL_TAKEN, v))
    sel = [jnp.zeros((LANES, TQ), F32) for _ in groups]
    for _ in range(min(NSA_N_SEL, nblk)):
        for g in groups:
            mx = jnp.max(imp[g], axis=0, keepdims=True)
            first = jnp.min(jnp.where(imp[g] == mx, blk, LANES), axis=0, keepdims=True)
            hit = blk == first
            sel[g] = jnp.where(hit, 1.0, sel[g])
            imp[g] = jnp.where(hit, SEL_TAKEN, imp[g])
    q_aug = []
    for g in groups:
        bias = jnp.where(sel[g] > 0, 0.0, SEL_MASK).T.astype(BF16)
        q_aug.append(jnp.concatenate([qs2[g], jnp.concatenate([bias] * R, axis=0)], axis=1))

    krow = lax.broadcasted_iota(jnp.int32, (TK, NQ), 0)
    acc_ref[...] = jnp.zeros_like(acc_ref)

    def sel_tile(j, ml, masked):
        s = [lax.dot_general(kaug_ref[g, j], q_aug[g], NT_DIMS, preferred_element_type=F32)
             for g in groups]
        out = []
        for g in groups:
            m_old, l_old = ml[2 * g], ml[2 * g + 1]
            sg = jnp.where(j * TK + krow <= tcol, s[g], -jnp.inf) if masked else s[g]
            m_new = jnp.maximum(m_old, jnp.max(sg, axis=0, keepdims=True))
            alpha = jnp.exp2(m_old - m_new)
            p = jnp.exp2(sg - m_new)
            l_new = alpha * l_old + jnp.sum(p, axis=0, keepdims=True)
            acc_ref[g] = alpha * acc_ref[g] + jnp.dot(vst_ref[g, j], p.astype(BF16),
                                                      preferred_element_type=F32)
            out += [m_new, l_new]
        return tuple(out)

    n_tiles = (t0 + TQ + TK - 1) // TK
    ml0 = (jnp.full((1, NQ), -jnp.inf, F32), jnp.zeros((1, NQ), F32)) * G
    ml = lax.fori_loop(0, n_tiles - 1, lambda j, c: sel_tile(j, c, False), ml0)
    ml = sel_tile(n_tiles - 1, ml, True)
    o_slc = [acc_ref[g] / ml[2 * g + 1] for g in groups]

    gt = _sigmoid(gt_ref[...]).T
    for g in groups:
        for r in range(R):
            c = 3 * (g * R + r)
            cols = slice(r * TQ, (r + 1) * TQ)
            o = (gt[c:c + 1] * o_cmp[g][:, cols] + gt[c + 1:c + 2] * o_slc[g][:, cols]
                 + gt[c + 2:c + 3] * o_win[g][:, cols])
            o_ref[:, (g * R + r) * d:(g * R + r + 1) * d] = o.T.astype(o_ref.dtype)


def nsa_attention(y3, k_cmp, v_cmp):
    batch, seq, n = y3.shape
    m = batch * seq
    TQ, TK, d = NSA_TQ, NSA_TK, NSA_DH
    R = NSA_HEADS // NSA_KV
    nq = seq // TQ
    ncp = k_cmp.shape[1]
    assert seq // NSA_SEL_LEN <= LANES and seq % TK == 0
    y2 = y3.reshape(m, n)
    v_cmp_t = v_cmp.transpose(0, 2, 1)
    nc = (seq - NSA_CMP_LEN) // NSA_CMP_STRIDE + 1
    cs = jnp.arange(ncp) * NSA_CMP_STRIDE
    ss = jnp.arange(LANES) * NSA_SEL_LEN
    overlap_t = ((cs[None, :] < ss[:, None] + NSA_SEL_LEN) & (cs[None, :] + NSA_CMP_LEN > ss[:, None])
                 & (jnp.arange(ncp)[None, :] < nc) & (jnp.arange(LANES)[:, None] < seq // NSA_SEL_LEN))
    overlap_t = overlap_t.astype(BF16)
    G = NSA_KV
    nk = seq // TK
    kaug, vst, kwb, vwt = nsa_prep(y3)
    whole = lambda shape: pl.BlockSpec((None,) + shape, lambda b, i: (b,) + (0,) * len(shape))
    return pl.pallas_call(
        functools.partial(_nsa_attn_kernel, seq=seq),
        out_shape=jax.ShapeDtypeStruct((m, MIX_HALF), BF16),
        grid=(batch, nq),
        in_specs=[
            pl.BlockSpec((TQ, NSA_HEADS * d), lambda b, i: (b * nq + i, 0)),
            pl.BlockSpec((TQ, LANES), lambda b, i: (b * nq + i, CD_GATE_COL // LANES)),
            whole((ncp, G * d)), whole((G * d, ncp)),
            whole((G, nk, TK, d + LANES)), whole((G, nk, d, TK)),
            whole((G, seq, d)), whole((G, seq // TQ, d, TQ)),
            pl.BlockSpec((LANES, ncp), lambda b, i: (0, 0)),
        ],
        out_specs=pl.BlockSpec((TQ, NSA_HEADS * d), lambda b, i: (b * nq + i, 0)),
        scratch_shapes=[pltpu.VMEM((G, d, R * TQ), F32)],
        compiler_params=_params(("parallel", "arbitrary")),
        name="nsa_attention",
    )(y2, y2, k_cmp, v_cmp_t, kaug, vst, kwb, vwt, overlap_t)


def _router_kernel(x_ref, nw_ref, w2_ref, wh_ref, bias_ref, o_ref):
    x = x_ref[...]
    xn = x * lax.rsqrt(jnp.mean(x * x, axis=-1, keepdims=True) + RMS_EPS) * nw_ref[...]
    xh = xn.astype(BF16)
    xl = (xn - xh.astype(F32)).astype(BF16)
    r1 = jnp.dot(xh, w2_ref[...], preferred_element_type=F32)
    r2 = jnp.dot(xl, wh_ref[...], preferred_element_type=F32)
    logits = r1[:, 0:LANES] + r1[:, LANES:] + r2 + bias_ref[...]
    lane = lax.broadcasted_iota(jnp.int32, logits.shape, 1)
    first_of = lambda hit: jnp.min(jnp.where(hit, lane, LANES), axis=-1, keepdims=True)
    gl = jnp.where(lane < MOE_GROUPS, logits, -jnp.inf)
    gmax = jnp.max(gl, axis=-1, keepdims=True)
    p_grp = 1.0 / jnp.sum(jnp.exp(gl - gmax), axis=-1, keepdims=True)
    g_idx = first_of(gl == gmax)
    e_lane = lane - MOE_GROUPS
    in_g = (e_lane >= 0) & (e_lane < MOE_EXPERTS) & ((e_lane >> 3) == g_idx)
    wl = jnp.where(in_g, logits, -jnp.inf)
    we = jnp.exp(wl - jnp.max(wl, axis=-1, keepdims=True))
    prob = jnp.where(in_g, we / jnp.sum(we, axis=-1, keepdims=True), -1.0)
    p0 = jnp.max(prob, axis=-1, keepdims=True)
    i0 = first_of(prob == p0)
    prob1 = jnp.where(lane == i0, -1.0, prob)
    p1 = jnp.max(prob1, axis=-1, keepdims=True)
    i1 = first_of(prob1 == p1)
    denom = p0 + p1
    vals = [(i0 - MOE_GROUPS).astype(F32), (i1 - MOE_GROUPS).astype(F32),
            p_grp * p0 / denom, p_grp * p1 / denom]
    out = jnp.zeros_like(logits)
    for c, v in enumerate(vals):
        out = jnp.where(lane == c, v, out)
    o_ref[...] = out


def router(h, nw, w_grp, b_grp, w_exp, b_exp, tm):
    m, d = h.shape
    assert MOE_EPG == 8
    wr = jnp.zeros((d, LANES), F32).at[:, 0:MOE_GROUPS].set(w_grp)
    wr = wr.at[:, MOE_GROUPS:MOE_GROUPS + MOE_EXPERTS].set(w_exp)
    bias = jnp.zeros((1, LANES), F32).at[0, 0:MOE_GROUPS].set(b_grp.astype(F32))
    bias = bias.at[0, MOE_GROUPS:MOE_GROUPS + MOE_EXPERTS].set(b_exp.astype(F32))
    wh = wr.astype(BF16)
    wl = (wr - wh.astype(F32)).astype(BF16)
    routed = pl.pallas_call(
        _router_kernel,
        out_shape=jax.ShapeDtypeStruct((m, LANES), F32),
        grid=(m // tm,),
        in_specs=[
            pl.BlockSpec((tm, d), lambda i: (i, 0)),
            pl.BlockSpec((1, d), lambda i: (0, 0)),
            pl.BlockSpec((d, 2 * LANES), lambda i: (0, 0)),
            pl.BlockSpec((d, LANES), lambda i: (0, 0)),
            pl.BlockSpec((1, LANES), lambda i: (0, 0)),
        ],
        out_specs=pl.BlockSpec((tm, LANES), lambda i: (i, 0)),
        compiler_params=_params(("parallel",)),
        name="router",
    )(h, nw.reshape(1, d), jnp.concatenate([wh, wl], axis=1), wh, bias)
    return routed[:, 0:MOE_TOPK].astype(jnp.int32), routed[:, MOE_TOPK:2 * MOE_TOPK]


def _cast3_kernel(a_ref, b_ref, c_ref, oa_ref, ob_ref, oc_ref):
    oa_ref[...] = a_ref[...].astype(oa_ref.dtype)
    ob_ref[...] = b_ref[...].astype(ob_ref.dtype)
    oc_ref[...] = c_ref[...].astype(oc_ref.dtype)


def expert_weights_bf16(w_gate, w_up, w_down, layer):
    _, E, d, dff = w_gate.shape
    split = 2
    in_spec = lambda r, c: pl.BlockSpec((None, None, r // split, c), lambda e, s: (layer, e, s, 0))
    out_spec = lambda r, c: pl.BlockSpec((None, r // split, c), lambda e, s: (e, s, 0))
    return pl.pallas_call(
        _cast3_kernel,
        out_shape=(jax.ShapeDtypeStruct((E, d, dff), BF16), jax.ShapeDtypeStruct((E, d, dff), BF16),
                   jax.ShapeDtypeStruct((E, dff, d), BF16)),
        grid=(E, split),
        in_specs=[in_spec(d, dff), in_spec(d, dff), in_spec(dff, d)],
        out_specs=(out_spec(d, dff), out_spec(d, dff), out_spec(dff, d)),
        compiler_params=_params(("parallel", "parallel")),
        name="expert_weights_bf16",
    )(w_gate, w_up, w_down)


SUBLANES = 8


def _gather_rows(idx_ref, base, src_ref, dst_ref, sem):
    def body(it, carry):
        for u in range(SUBLANES):
            tok = idx_ref[base + it * SUBLANES + u]
            pltpu.make_async_copy(src_ref.at[pl.ds(tok, 1)], dst_ref.at[it, pl.ds(u, 1)], sem).start()
        return carry
    lax.fori_loop(0, dst_ref.shape[0], body, 0)


def _wait_rows(dst_ref, sem):
    pltpu.make_async_copy(dst_ref, dst_ref, sem).wait()


def _expert_kernel(be_ref, nused_ref, tok_ref, h_ref, nw_ref, wg_ref, wu_ref, wd_ref, o_ref,
                   xbuf, sems):
    TM = MOE_TM
    i = pl.program_id(0)
    n_used = nused_ref[0]
    slot = i % 2

    @pl.when(i == 0)
    def _():
        _gather_rows(tok_ref, 0, h_ref, xbuf.at[0], sems.at[0])

    @pl.when(i + 1 < n_used)
    def _():
        _gather_rows(tok_ref, (i + 1) * TM, h_ref, xbuf.at[1 - slot], sems.at[1 - slot])

    @pl.when(i < n_used)
    def _():
        _wait_rows(xbuf.at[slot], sems.at[slot])
        x = xbuf[slot].reshape(TM, xbuf.shape[-1])
        xn = (x * lax.rsqrt(jnp.mean(x * x, axis=-1, keepdims=True) + RMS_EPS) * nw_ref[...]).astype(BF16)
        hg = jnp.dot(xn, wg_ref[...], preferred_element_type=F32)
        hu = jnp.dot(xn, wu_ref[...], preferred_element_type=F32)
        hb = (_silu(hg) * hu).astype(BF16)
        o_ref[...] = jnp.dot(hb, wd_ref[...], preferred_element_type=F32)

    @pl.when(i >= n_used)
    def _():
        o_ref[...] = jnp.zeros_like(o_ref)


def expert_ffn(h, nw, row_tok, blk_expert, n_used, w_gate, w_up, w_down):
    m, d = h.shape
    rows = row_tok.shape[0]
    TM = MOE_TM
    nb = rows // TM
    dff = w_gate.shape[2]
    grid_spec = pltpu.PrefetchScalarGridSpec(
        num_scalar_prefetch=3,
        grid=(nb,),
        in_specs=[
            pl.BlockSpec(memory_space=pl.ANY),
            pl.BlockSpec((1, d), lambda i, be, nu, tok: (0, 0)),
            pl.BlockSpec((None, d, dff), lambda i, be, nu, tok: (be[i], 0, 0)),
            pl.BlockSpec((None, d, dff), lambda i, be, nu, tok: (be[i], 0, 0)),
            pl.BlockSpec((None, dff, d), lambda i, be, nu, tok: (be[i], 0, 0)),
        ],
        out_specs=pl.BlockSpec((TM, d), lambda i, be, nu, tok: (i, 0)),
        scratch_shapes=[pltpu.VMEM((2, TM // SUBLANES, SUBLANES, d), F32), pltpu.SemaphoreType.DMA((2,))],
    )
    return pl.pallas_call(
        _expert_kernel,
        out_shape=jax.ShapeDtypeStruct((rows, d), F32),
        grid_spec=grid_spec,
        compiler_params=_params(("arbitrary",)),
        name="expert_ffn",
    )(blk_expert, n_used, row_tok, h, nw.reshape(1, d), w_gate, w_up, w_down)


def _combine_kernel(dest_ref, ys_ref, h_ref, gate_ref, fw_ref, o_ref, ybuf, sems, *, final_norm):
    tc = h_ref.shape[0]
    i = pl.program_id(0)
    n = pl.num_programs(0)
    slot = i % 2

    @pl.when(i == 0)
    def _():
        _gather_rows(dest_ref, 0, ys_ref, ybuf.at[0], sems.at[0])

    @pl.when(i + 1 < n)
    def _():
        _gather_rows(dest_ref, (i + 1) * 2 * tc, ys_ref, ybuf.at[1 - slot], sems.at[1 - slot])

    _wait_rows(ybuf.at[slot], sems.at[slot])
    gate = gate_ref[...]
    d = h_ref.shape[1]
    g8 = tc // SUBLANES
    y0 = ybuf[slot, 0:g8].reshape(tc, d)
    y1 = ybuf[slot, g8:].reshape(tc, d)
    out = h_ref[...] + (y0 * gate[:, 0:1] + y1 * gate[:, 1:2])
    if final_norm:
        out = out * lax.rsqrt(jnp.mean(out * out, axis=-1, keepdims=True) + RMS_EPS) * fw_ref[...]
    o_ref[...] = out


def moe_combine(h, ys, dest_tiles, gate, final_w, tc, final_norm):
    m, d = h.shape
    grid_spec = pltpu.PrefetchScalarGridSpec(
        num_scalar_prefetch=1,
        grid=(m // tc,),
        in_specs=[
            pl.BlockSpec(memory_space=pl.ANY),
            pl.BlockSpec((tc, d), lambda i, dst: (i, 0)),
            pl.BlockSpec((tc, MOE_TOPK), lambda i, dst: (i, 0)),
            pl.BlockSpec((1, d), lambda i, dst: (0, 0)),
        ],
        out_specs=pl.BlockSpec((tc, d), lambda i, dst: (i, 0)),
        scratch_shapes=[pltpu.VMEM((2, 2 * tc // SUBLANES, SUBLANES, d), F32),
                        pltpu.SemaphoreType.DMA((2,))],
    )
    return pl.pallas_call(
        functools.partial(_combine_kernel, final_norm=final_norm),
        out_shape=jax.ShapeDtypeStruct((m, d), F32),
        grid_spec=grid_spec,
        compiler_params=_params(("arbitrary",)),
        name="moe_combine",
    )(dest_tiles, ys, h, gate, final_w.reshape(1, d))


def _dispatch_tables(expert):
    n = expert.shape[0]
    K, E, TM = MOE_TOPK, MOE_EXPERTS, MOE_TM
    A = n * K
    flat_e = expert.reshape(A)
    iota = jnp.arange(A, dtype=jnp.int32)
    e_sorted, order = lax.sort((flat_e, iota), num_keys=1)
    counts = jnp.sum((flat_e[:, None] == jnp.arange(E, dtype=jnp.int32)[None, :]).astype(jnp.int32), axis=0)
    padded = (counts + TM - 1) // TM * TM
    pad_end = jnp.cumsum(padded)
    pad_start = pad_end - padded
    start = jnp.cumsum(counts) - counts
    dest_sorted = (pad_start[e_sorted] + iota - start[e_sorted]).astype(jnp.int32)
    _, dest = lax.sort((order, dest_sorted), num_keys=1)
    rows = A + E * TM
    nb = rows // TM
    blk_expert = jnp.minimum(jnp.sum((jnp.arange(nb, dtype=jnp.int32)[:, None] * TM >= pad_end[None, :])
                                     .astype(jnp.int32), axis=1), E - 1).astype(jnp.int32)
    r = jnp.arange(rows, dtype=jnp.int32)
    e_row = jnp.repeat(blk_expert, TM)
    pos = r - pad_start[e_row].astype(jnp.int32)
    src = jnp.clip(start[e_row].astype(jnp.int32) + pos, 0, A - 1)
    row_tok = jnp.where(pos < counts[e_row], order[src] // K, 0).astype(jnp.int32)
    n_used = (pad_end[-1] // TM).astype(jnp.int32).reshape(1)
    return dest.reshape(n, K), row_tok, blk_expert, n_used


def hier_moe(h, nw, w_grp, b_grp, w_exp, b_exp, w_gate_all, w_up_all, w_down_all, layer, final_w,
             final_norm):
    m, d = h.shape
    tc = 256
    expert, gate = router(h, nw, w_grp, b_grp, w_exp, b_exp, 512)
    dest, row_tok, blk_expert, n_used = _dispatch_tables(expert)
    w_gate, w_up, w_down = expert_weights_bf16(w_gate_all, w_up_all, w_down_all, layer)
    ys = expert_ffn(h, nw, row_tok, blk_expert, n_used, w_gate, w_up, w_down)
    dest_tiles = dest.reshape(m // tc, tc, MOE_TOPK).transpose(0, 2, 1).reshape(-1)
    return moe_combine(h, ys, dest_tiles, gate, final_w, tc, final_norm)


def _cd_in_weights(w):
    gates = w[:, 2560:2584]
    pad = jnp.zeros((w.shape[0], CD_N - CD_GATE_COL - gates.shape[1]), w.dtype)
    return jnp.concatenate([w[:, 0:2560], w[:, 2584:], gates, pad], axis=1)


def mixer_ab(h, nw, w_in, w_out, lb, hg_norm, sw_sinks, batch, seq):
    y = norm_matmul(h, nw, w_in.astype(BF16), min(1024, h.shape[0]), 1408)
    o_a = hgrn2(y, lb, hg_norm, batch, seq, min(512, seq))
    o_b = swa(y, sw_sinks, batch, seq)
    return outproj_residual(o_a, o_b, w_out.astype(BF16), h, 512)


def mixer_cd(h, nw, w_in, w_out, pe_k, w1_k, w2_k, pe_v, w1_v, w2_v, ret_norm, batch, seq):
    y = norm_matmul(h, nw, _cd_in_weights(w_in).astype(BF16), min(1024, h.shape[0]), 1152)
    y3 = y.reshape(batch, seq, CD_N)
    k_cmp, v_cmp = nsa_compress(y3, pe_k, w1_k, w2_k, pe_v, w1_v, w2_v)
    o_c = nsa_attention(y3, k_cmp, v_cmp)
    o_d = retention(y, ret_norm, batch, seq, min(512, seq))
    return outproj_residual(o_c, o_d, w_out.astype(BF16), h, 512)


def kernel(x, norm_mix, norm_ffn, norm_final, ab_w_in, ab_w_out, hg_lb_logits, hg_norm, sw_sinks, cd_w_in, cd_w_out, nsa_pe_k, nsa_w1_k, nsa_w2_k, nsa_pe_v, nsa_w1_v, nsa_w2_v, ret_norm, moe_w_grp, moe_b_grp, moe_w_exp, moe_b_exp, moe_w_gate, moe_w_up, moe_w_down):
    batch, seq, d = x.shape
    depth = norm_mix.shape[0]
    lb_all = jnp.cumsum(jax.nn.softmax(hg_lb_logits.astype(F32), axis=0), axis=0)
    h = x.reshape(batch * seq, d)
    for layer in range(depth):
        if layer % 2 == 0:
            e = layer // 2
            h = mixer_ab(h, norm_mix[layer], ab_w_in[e], ab_w_out[e], lb_all[layer], hg_norm[e],
                         sw_sinks[e], batch, seq)
        else:
            o = layer // 2
            h = mixer_cd(h, norm_mix[layer], cd_w_in[o], cd_w_out[o], nsa_pe_k[o], nsa_w1_k[o],
                         nsa_w2_k[o], nsa_pe_v[o], nsa_w1_v[o], nsa_w2_v[o], ret_norm[o], batch, seq)
        h = hier_moe(h, norm_ffn[layer], moe_w_grp[layer], moe_b_grp[layer], moe_w_exp[layer],
                     moe_b_exp[layer], moe_w_gate, moe_w_up, moe_w_down, layer,
                     norm_final, layer == depth - 1)
    return h.reshape(batch, seq, d)
```

```python
import functools

import jax
import jax.numpy as jnp
from jax import lax
from jax.experimental import pallas as pl
from jax.experimental.pallas import tpu as pltpu

F32 = jnp.float32
BF16 = jnp.bfloat16

D_MODEL = 2048
MIX_HALF = D_MODEL // 2
RMS_EPS = 1e-6
LANES = 128
VMEM_LIMIT = 56 * 1024 * 1024

HG_DK = 128
HG_HEADS = 8
HG_CHUNK = 64
HG_SUB = 16
SW_DH = 128
SW_HEADS = 8
SW_KV = 2
SW_WINDOW = 128
NSA_DH = 128
NSA_HEADS = 8
NSA_KV = 2
NSA_CMP_LEN = 32
NSA_CMP_STRIDE = 16
NSA_CMP_HIDDEN = 128
NSA_SEL_LEN = 64
NSA_N_SEL = 8
NSA_WINDOW = 512
RET_DK = 128
RET_DV = 256
RET_HEADS = 4
RET_CHUNK = 128
RET_THETA_BASE = 10000.0
MOE_GROUPS = 4
MOE_EPG = 8
MOE_EXPERTS = 32
MOE_TOPK = 2
MOE_DFF = 1024
MOE_TM = 256

AB_N = 5632
CD_N = 5760
CD_GATE_COL = 5632

NT_DIMS = (((1,), (1,)), ((), ()))
TN_DIMS = (((0,), (0,)), ((), ()))


def _params(sem, vmem=VMEM_LIMIT):
    return pltpu.CompilerParams(dimension_semantics=sem, vmem_limit_bytes=vmem)


def _sigmoid(x):
    return 1.0 / (1.0 + jnp.exp(-x))


def _silu(x):
    return x * _sigmoid(x)


def _dot_exact01(a01, x):
    n = x.shape[1]
    hi = x.astype(BF16)
    rem = x - hi.astype(F32)
    mid = rem.astype(BF16)
    lo = (rem - mid.astype(F32)).astype(BF16)
    r = jnp.dot(a01, jnp.concatenate([hi, mid, lo], axis=1), preferred_element_type=F32)
    return r[:, 0:n] + r[:, n:2 * n] + r[:, 2 * n:]


def _norm_matmul_kernel(x_ref, nw_ref, w_ref, o_ref, xn_ref):
    @pl.when(pl.program_id(1) == 0)
    def _():
        x = x_ref[...]
        ms = jnp.mean(x * x, axis=-1, keepdims=True)
        xn_ref[...] = (x * lax.rsqrt(ms + RMS_EPS) * nw_ref[...]).astype(BF16)

    o_ref[...] = jnp.dot(xn_ref[...], w_ref[...], preferred_element_type=F32)


def norm_matmul(x, nw, w, tm, tn):
    m, d = x.shape
    n = w.shape[1]
    return pl.pallas_call(
        _norm_matmul_kernel,
        out_shape=jax.ShapeDtypeStruct((m, n), F32),
        grid=(m // tm, n // tn),
        in_specs=[
            pl.BlockSpec((tm, d), lambda i, j: (i, 0)),
            pl.BlockSpec((1, d), lambda i, j: (0, 0)),
            pl.BlockSpec((d, tn), lambda i, j: (0, j)),
        ],
        out_specs=pl.BlockSpec((tm, tn), lambda i, j: (i, j)),
        scratch_shapes=[pltpu.VMEM((tm, d), BF16)],
        compiler_params=_params(("parallel", "arbitrary")),
        name="norm_matmul",
    )(x, nw.reshape(1, d), w)


def _outproj_kernel(a_ref, b_ref, w_ref, h_ref, o_ref):
    half = a_ref.shape[1]
    acc = jnp.dot(a_ref[...], w_ref[0:half, :], preferred_element_type=F32)
    acc = acc + jnp.dot(b_ref[...], w_ref[half:, :], preferred_element_type=F32)
    o_ref[...] = h_ref[...] + acc


def outproj_residual(a, b, w, h, tm):
    m, half = a.shape
    d = w.shape[1]
    return pl.pallas_call(
        _outproj_kernel,
        out_shape=jax.ShapeDtypeStruct((m, d), F32),
        grid=(m // tm,),
        in_specs=[
            pl.BlockSpec((tm, half), lambda i: (i, 0)),
            pl.BlockSpec((tm, half), lambda i: (i, 0)),
            pl.BlockSpec((2 * half, d), lambda i: (0, 0)),
            pl.BlockSpec((tm, d), lambda i: (i, 0)),
        ],
        out_specs=pl.BlockSpec((tm, d), lambda i: (i, 0)),
        compiler_params=_params(("parallel",)),
        name="outproj_residual",
    )(a, b, w, h)


def _hgrn2_kernel(q_ref, f_ref, i_ref, g_ref, lb_ref, nw_ref, o_ref, st_ref, *, n_chunks):
    C, c = HG_CHUNK, HG_SUB
    nsub = C // c

    @pl.when(pl.program_id(2) == 0)
    def _():
        st_ref[...] = jnp.zeros_like(st_ref)

    lb = lb_ref[...]
    nw = nw_ref[...]
    tri = (lax.broadcasted_iota(jnp.int32, (C, C), 0)
           >= lax.broadcasted_iota(jnp.int32, (C, C), 1)).astype(BF16)
    hs = c // 2
    row8 = lax.broadcasted_iota(jnp.int32, (1, hs, 1), 1)
    chunks = range(n_chunks)
    rows_of = lambda ci: slice(ci * C, (ci + 1) * C)

    q = [q_ref[rows_of(ci), :] for ci in chunks]
    v = [i_ref[rows_of(ci), :] for ci in chunks]
    vb = [x.astype(BF16) for x in v]
    f = [lb + (1.0 - lb) * _sigmoid(f_ref[rows_of(ci), :]) for ci in chunks]
    k = [1.0 - x for x in f]
    b = [_dot_exact01(tri, jnp.log(x)) for x in f]

    a_off = []
    for ci in chunks:
        for i in range(1, nsub):
            r = b[ci][i * c - 1:i * c, :]
            qi = q[ci][i * c:(i + 1) * c] * jnp.exp(b[ci][i * c:(i + 1) * c] - r)
            ki = k[ci][0:i * c] * jnp.exp(r - b[ci][0:i * c])
            a_off.append(lax.dot_general(qi.astype(BF16), ki.astype(BF16), NT_DIMS,
                                         preferred_element_type=F32))
    o_off = []
    for ci in chunks:
        rows = [jnp.zeros((c, HG_DK), F32)]
        for i in range(1, nsub):
            a = a_off[ci * (nsub - 1) + i - 1]
            rows.append(jnp.dot(a.astype(BF16), vb[ci][0:i * c], preferred_element_type=F32))
        o_off.append(jnp.concatenate(rows, axis=0))
    inc = []
    for ci in chunks:
        bend = b[ci][C - 1:C, :]
        kd = k[ci] * jnp.exp(bend - b[ci])
        inc.append(lax.dot_general(vb[ci], kd.astype(BF16), TN_DIMS, preferred_element_type=F32))
    st = st_ref[...]
    o_st = []
    for ci in chunks:
        o_st.append(lax.dot_general((q[ci] * jnp.exp(b[ci])).astype(BF16), st.astype(BF16), NT_DIMS,
                                    preferred_element_type=F32))
        st = st * jnp.exp(b[ci][C - 1:C, :]) + inc[ci]
    st_ref[...] = st

    for ci in chunks:
        b4 = b[ci].reshape(nsub, c, HG_DK)
        q4 = q[ci].reshape(nsub, c, HG_DK)
        k4 = k[ci].reshape(nsub, c, HG_DK)
        v4 = v[ci].reshape(nsub, c, HG_DK)
        bt, bb = b4[:, 0:hs], b4[:, hs:]
        qt, qb = q4[:, 0:hs], q4[:, hs:]
        od_t = jnp.zeros((nsub, hs, HG_DK), F32)
        od_b = jnp.zeros((nsub, hs, HG_DK), F32)
        for s in range(c):
            piv, kp, vp = b4[:, s:s + 1], k4[:, s:s + 1], v4[:, s:s + 1]
            if s < hs:
                e = jnp.exp(jnp.minimum(bt - piv, 0.0))
                a = jnp.sum(qt * e * kp, axis=-1, keepdims=True)
                od_t = od_t + jnp.where(row8 >= s, a, 0.0) * vp
                e = jnp.exp(jnp.minimum(bb - piv, 0.0))
                od_b = od_b + jnp.sum(qb * e * kp, axis=-1, keepdims=True) * vp
            else:
                e = jnp.exp(jnp.minimum(bb - piv, 0.0))
                a = jnp.sum(qb * e * kp, axis=-1, keepdims=True)
                od_b = od_b + jnp.where(row8 >= s - hs, a, 0.0) * vp
        o = o_off[ci] + o_st[ci] + jnp.concatenate([od_t, od_b], axis=1).reshape(C, HG_DK)
        y = o * lax.rsqrt(jnp.mean(o * o, axis=-1, keepdims=True) + RMS_EPS) * nw
        o_ref[rows_of(ci), :] = (y * _silu(g_ref[rows_of(ci), :])).astype(o_ref.dtype)


def hgrn2(y, lb, nw, batch, seq, tc):
    m = y.shape[0]
    nt = seq // tc
    row = lambda b, h, t: b * nt + t
    hd = HG_HEADS
    return pl.pallas_call(
        functools.partial(_hgrn2_kernel, n_chunks=tc // HG_CHUNK),
        out_shape=jax.ShapeDtypeStruct((m, MIX_HALF), BF16),
        grid=(batch, hd, nt),
        in_specs=[
            pl.BlockSpec((tc, HG_DK), lambda b, h, t: (row(b, h, t), h)),
            pl.BlockSpec((tc, HG_DK), lambda b, h, t: (row(b, h, t), hd + h)),
            pl.BlockSpec((tc, HG_DK), lambda b, h, t: (row(b, h, t), 2 * hd + h)),
            pl.BlockSpec((tc, HG_DK), lambda b, h, t: (row(b, h, t), 3 * hd + h)),
            pl.BlockSpec((None, 1, HG_DK), lambda b, h, t: (h, 0, 0)),
            pl.BlockSpec((None, 1, HG_DK), lambda b, h, t: (h, 0, 0)),
        ],
        out_specs=pl.BlockSpec((tc, HG_DK), lambda b, h, t: (row(b, h, t), h)),
        scratch_shapes=[pltpu.VMEM((HG_DK, HG_DK), F32)],
        compiler_params=_params(("parallel", "parallel", "arbitrary")),
        name="hgrn2",
    )(y, y, y, y, lb.reshape(hd, 1, HG_DK), nw.reshape(hd, 1, HG_DK))


def _swa_kernel(q_ref, kp_ref, kc_ref, vp_ref, vc_ref, sink_ref, o_ref):
    W, d = SW_WINDOW, SW_DH
    G = SW_KV
    R = SW_HEADS // G
    NQ = R * W
    groups = range(G)
    n = pl.program_id(1)
    q = q_ref[...]
    qs = [(jnp.concatenate([q[:, (g * R + r) * d:(g * R + r + 1) * d] for r in range(R)], axis=0)
           * (d ** -0.5 * LOG2E)).astype(BF16) for g in groups]
    kk = [jnp.concatenate([kp_ref[:, g * d:(g + 1) * d], kc_ref[:, g * d:(g + 1) * d]],
                          axis=0).astype(BF16) for g in groups]
    vt = [jnp.concatenate([vp_ref[:, g * d:(g + 1) * d].T, vc_ref[:, g * d:(g + 1) * d].T],
                          axis=1).astype(BF16) for g in groups]
    s = [lax.dot_general(kk[g], qs[g], NT_DIMS, preferred_element_type=F32) for g in groups]
    kj = lax.broadcasted_iota(jnp.int32, (2 * W, NQ), 0)
    rel = (lax.broadcasted_iota(jnp.int32, (2 * W, NQ), 1) & (W - 1)) + W - kj
    valid = (rel >= 0) & (rel < W) & ((kj >= W) | (n > 0))
    outs = []
    for g in groups:
        sg = jnp.where(valid, s[g], -jnp.inf)
        sink = sink_ref[g] * LOG2E
        mx = jnp.maximum(jnp.max(sg, axis=0, keepdims=True), sink)
        p = jnp.exp2(sg - mx)
        z = jnp.sum(p, axis=0, keepdims=True) + jnp.exp2(sink - mx)
        outs.append(jnp.dot(vt[g], p.astype(BF16), preferred_element_type=F32) / z)
    for g in groups:
        for r in range(R):
            o_ref[:, (g * R + r) * d:(g * R + r + 1) * d] = outs[g][:, r * W:(r + 1) * W].T.astype(o_ref.dtype)


def swa(y, sinks, batch, seq):
    m = y.shape[0]
    W, d, G = SW_WINDOW, SW_DH, SW_KV
    R = SW_HEADS // G
    nb = seq // W
    qcol = 4 * MIX_HALF // (SW_HEADS * d)
    kcol = (4 * MIX_HALF + SW_HEADS * d) // (G * d)
    vcol = kcol + 1
    cur = lambda b, n: b * nb + n
    prev = lambda b, n: b * nb + jnp.maximum(n - 1, 0)
    sink_cols = jnp.repeat(sinks.astype(F32).reshape(G, R), W, axis=1).reshape(G, 1, R * W)
    return pl.pallas_call(
        _swa_kernel,
        out_shape=jax.ShapeDtypeStruct((m, MIX_HALF), BF16),
        grid=(batch, nb),
        in_specs=[
            pl.BlockSpec((W, SW_HEADS * d), lambda b, n: (cur(b, n), qcol)),
            pl.BlockSpec((W, G * d), lambda b, n: (prev(b, n), kcol)),
            pl.BlockSpec((W, G * d), lambda b, n: (cur(b, n), kcol)),
            pl.BlockSpec((W, G * d), lambda b, n: (prev(b, n), vcol)),
            pl.BlockSpec((W, G * d), lambda b, n: (cur(b, n), vcol)),
            pl.BlockSpec((G, 1, R * W), lambda b, n: (0, 0, 0)),
        ],
        out_specs=pl.BlockSpec((W, SW_HEADS * d), lambda b, n: (cur(b, n), 0)),
        compiler_params=_params(("parallel", "parallel")),
        name="swa",
    )(y, y, y, y, y, sink_cols)


def _retention_kernel(q_ref, k_ref, v_ref, g_ref, cos_ref, sin_ref, dmat_ref, xi_ref, zeta_ref,
                      cdec_ref, nw_ref, o_ref, st_ref, *, n_chunks):
    C = RET_CHUNK

    @pl.when(pl.program_id(2) == 0)
    def _():
        st_ref[...] = jnp.zeros_like(st_ref)

    dmat = dmat_ref[...]
    xi = xi_ref[...]
    zeta = zeta_ref[...]
    cdec = cdec_ref[...]
    nw = nw_ref[...]

    chunks = range(n_chunks)
    rows_of = lambda ci: slice(ci * C, (ci + 1) * C)

    def rotate(x_ref, ci):
        x = x_ref[rows_of(ci), :]
        return x * cos_ref[rows_of(ci), :] + pltpu.roll(x, RET_DK // 2, 1) * sin_ref[rows_of(ci), :]

    qr = [rotate(q_ref, ci) for ci in chunks]
    kr = [rotate(k_ref, ci) * (RET_DK ** -0.5) for ci in chunks]
    vb = [v_ref[rows_of(ci), :].astype(BF16) for ci in chunks]
    inner = [lax.dot_general(qr[ci].astype(BF16), kr[ci].astype(BF16), NT_DIMS,
                             preferred_element_type=F32) * dmat for ci in chunks]
    inc = [lax.dot_general((kr[ci] * zeta).astype(BF16), vb[ci], TN_DIMS,
                           preferred_element_type=F32) for ci in chunks]
    o_in = [jnp.dot(inner[ci].astype(BF16), vb[ci], preferred_element_type=F32) for ci in chunks]
    st = st_ref[...]
    o_cross = []
    for ci in chunks:
        o_cross.append(jnp.dot((qr[ci] * xi).astype(BF16), st.astype(BF16), preferred_element_type=F32))
        st = cdec * st + inc[ci]
    st_ref[...] = st
    for ci in chunks:
        o = o_in[ci] + o_cross[ci]
        cen = o - jnp.mean(o, axis=-1, keepdims=True)
        y = cen * lax.rsqrt(jnp.mean(cen * cen, axis=-1, keepdims=True) + RMS_EPS) * nw
        o_ref[rows_of(ci), :] = (y * _silu(g_ref[rows_of(ci), :])).astype(o_ref.dtype)


def retention(y, nw, batch, seq, tc):
    m = y.shape[0]
    nt = seq // tc
    H, C = RET_HEADS, RET_CHUNK
    log_g = jnp.log1p(-jnp.exp2(-5.0 - jnp.arange(H, dtype=F32)))
    j = jnp.arange(C, dtype=F32)
    diff = j[:, None] - j[None, :]
    dmat = jnp.where(diff >= 0, jnp.exp(jnp.maximum(diff, 0.0)[None] * log_g[:, None, None]), 0.0)
    zeta = jnp.exp((C - 1 - j)[None, :] * log_g[:, None]).reshape(H, C, 1)
    xi = jnp.exp((j + 1.0)[None, :] * log_g[:, None]).reshape(H, C, 1)
    cdec = jnp.broadcast_to(jnp.exp(C * log_g)[:, None, None], (H, 1, RET_DV))
    half = RET_DK // 2
    theta = 1.0 / (RET_THETA_BASE ** jnp.linspace(0.0, 1.0, half, dtype=F32))
    ang = jnp.arange(seq).astype(F32)[:, None] * theta[None, :]
    cos2 = jnp.concatenate([jnp.cos(ang), jnp.cos(ang)], axis=-1)
    sin2 = jnp.concatenate([-jnp.sin(ang), jnp.sin(ang)], axis=-1)

    row = lambda b, h, t: b * nt + t
    qcol = 2560 // RET_DK
    kcol = 3072 // RET_DK
    vcol = 3584 // RET_DV
    gcol = 4608 // RET_DV
    return pl.pallas_call(
        functools.partial(_retention_kernel, n_chunks=tc // C),
        out_shape=jax.ShapeDtypeStruct((m, MIX_HALF), BF16),
        grid=(batch, H, nt),
        in_specs=[
            pl.BlockSpec((tc, RET_DK), lambda b, h, t: (row(b, h, t), qcol + h)),
            pl.BlockSpec((tc, RET_DK), lambda b, h, t: (row(b, h, t), kcol + h)),
            pl.BlockSpec((tc, RET_DV), lambda b, h, t: (row(b, h, t), vcol + h)),
            pl.BlockSpec((tc, RET_DV), lambda b, h, t: (row(b, h, t), gcol + h)),
            pl.BlockSpec((tc, RET_DK), lambda b, h, t: (t, 0)),
            pl.BlockSpec((tc, RET_DK), lambda b, h, t: (t, 0)),
            pl.BlockSpec((None, C, C), lambda b, h, t: (h, 0, 0)),
            pl.BlockSpec((None, C, 1), lambda b, h, t: (h, 0, 0)),
            pl.BlockSpec((None, C, 1), lambda b, h, t: (h, 0, 0)),
            pl.BlockSpec((None, 1, RET_DV), lambda b, h, t: (h, 0, 0)),
            pl.BlockSpec((None, 1, RET_DV), lambda b, h, t: (h, 0, 0)),
        ],
        out_specs=pl.BlockSpec((tc, RET_DV), lambda b, h, t: (row(b, h, t), h)),
        scratch_shapes=[pltpu.VMEM((RET_DK, RET_DV), F32)],
        compiler_params=_params(("parallel", "parallel", "arbitrary")),
        name="retention",
    )(y, y, y, y, cos2, sin2, dmat, xi, zeta, cdec, nw.reshape(H, 1, RET_DV))


def _gelu_tanh(x):
    return 0.5 * x * (1.0 + jnp.tanh(0.7978845608028654 * (x + 0.044715 * x * x * x)))


def _nsa_compress_kernel(xk_ref, xv_ref, pek_ref, pev_ref, w1k_ref, w1v_ref, w2k_ref, w2v_ref,
                         ok_ref, ov_ref):
    gh = NSA_KV * NSA_CMP_HIDDEN

    def one(x_ref, pe_ref, w1_ref, w2_ref, o_ref):
        w1 = w1_ref[...]
        pq = jnp.dot(x_ref[...].astype(BF16), w1, preferred_element_type=F32)
        pe = jnp.dot(pe_ref[...].astype(BF16), w1, preferred_element_type=F32)
        const = pe[0:1, 0:gh] + pe[1:2, gh:]
        nrow = pq.shape[0]
        nxt = pltpu.roll(pq[:, gh:], nrow - 1, 0)
        hid = _gelu_tanh(pq[:, 0:gh] + nxt + const)
        o_ref[...] = jnp.dot(hid.astype(BF16), w2_ref[...], preferred_element_type=F32)

    one(xk_ref, pek_ref, w1k_ref, w2k_ref, ok_ref)
    one(xv_ref, pev_ref, w1v_ref, w2v_ref, ov_ref)


def _compress_weights(pe, w1, w2):
    G, L, S = NSA_KV, NSA_CMP_LEN, NSA_CMP_STRIDE
    d, hd = NSA_DH, NSA_CMP_HIDDEN
    eye = jnp.eye(G, dtype=F32)

    def big(w):
        return jnp.einsum('ldh,ge->lgdeh', w, eye).reshape(S * G * d, G * hd)

    w1big = jnp.concatenate([big(w1[:S]), big(w1[S:])], axis=1).astype(BF16)
    w2big = jnp.einsum('hd,ge->ghed', w2, eye).reshape(G * hd, G * d).astype(BF16)

    def pe_row(p):
        return jnp.broadcast_to(p[:, None, :], (S, G, d)).reshape(S * G * d)

    pe8 = jnp.zeros((8, S * G * d), F32).at[0].set(pe_row(pe[:S])).at[1].set(pe_row(pe[S:]))
    return pe8, w1big, w2big


def nsa_compress(y3, pe_k, w1_k, w2_k, pe_v, w1_v, w2_v):
    batch, seq, _ = y3.shape
    G, S, d = NSA_KV, NSA_CMP_STRIDE, NSA_DH
    nrow = seq // S
    xk = y3[:, :, 1024:1280].reshape(batch, nrow, S * G * d)
    xv = y3[:, :, 1280:1536].reshape(batch, nrow, S * G * d)
    pek, w1k, w2k = _compress_weights(pe_k, w1_k, w2_k)
    pev, w1v, w2v = _compress_weights(pe_v, w1_v, w2_v)
    kdim = S * G * d
    full = lambda shape: pl.BlockSpec(shape, lambda b: (0,) * len(shape))
    out = jax.ShapeDtypeStruct((batch, nrow, G * d), F32)
    return pl.pallas_call(
        _nsa_compress_kernel,
        out_shape=(out, out),
        grid=(batch,),
        in_specs=[
            pl.BlockSpec((None, nrow, kdim), lambda b: (b, 0, 0)),
            pl.BlockSpec((None, nrow, kdim), lambda b: (b, 0, 0)),
            full((8, kdim)), full((8, kdim)),
            full(w1k.shape), full(w1v.shape), full(w2k.shape), full(w2v.shape),
        ],
        out_specs=(pl.BlockSpec((None, nrow, G * d), lambda b: (b, 0, 0)),
                   pl.BlockSpec((None, nrow, G * d), lambda b: (b, 0, 0))),
        compiler_params=_params(("parallel",)),
        name="nsa_compress",
    )(xk, xv, pek, pev, w1k, w1v, w2k, w2v)


NSA_TQ = 128
NSA_TK = 512
SEL_SHIFT = 6
SEL_FORCED = 1e30
SEL_FUTURE = -1e30
SEL_TAKEN = -3e38
SEL_MASK = -1e9
LOG2E = 1.4426950408889634


def _nsa_prep_kernel(ks_ref, vs_ref, kw_ref, vw_ref, kaug_ref, vst_ref, kwb_ref, vwt_ref):
    TK, TQ = NSA_TK, NSA_TQ
    k0 = pl.program_id(2) * TK
    krow = lax.broadcasted_iota(jnp.int32, (TK, LANES), 0)
    bcol = lax.broadcasted_iota(jnp.int32, (TK, LANES), 1)
    onehot = jnp.where(((k0 + krow) >> SEL_SHIFT) == bcol, 1.0, 0.0).astype(BF16)
    kaug_ref[...] = jnp.concatenate([ks_ref[...].astype(BF16), onehot], axis=1)
    vst_ref[...] = vs_ref[...].T.astype(BF16)
    kwb_ref[...] = kw_ref[...].astype(BF16)
    for h in range(TK // TQ):
        vwt_ref[h] = vw_ref[h * TQ:(h + 1) * TQ, :].T.astype(BF16)


def nsa_prep(y3):
    batch, seq, _ = y3.shape
    G, d, TK, TQ = NSA_KV, NSA_DH, NSA_TK, NSA_TQ
    nk = seq // TK
    src = lambda col: pl.BlockSpec((None, TK, d), lambda b, g, c: (b, c, col + g))
    return pl.pallas_call(
        _nsa_prep_kernel,
        out_shape=(jax.ShapeDtypeStruct((batch, G, nk, TK, d + LANES), BF16),
                   jax.ShapeDtypeStruct((batch, G, nk, d, TK), BF16),
                   jax.ShapeDtypeStruct((batch, G, seq, d), BF16),
                   jax.ShapeDtypeStruct((batch, G, seq // TQ, d, TQ), BF16)),
        grid=(batch, G, nk),
        in_specs=[src(1536 // d), src(1792 // d), src(2048 // d), src(2304 // d)],
        out_specs=(pl.BlockSpec((None, None, None, TK, d + LANES), lambda b, g, c: (b, g, c, 0, 0)),
                   pl.BlockSpec((None, None, None, d, TK), lambda b, g, c: (b, g, c, 0, 0)),
                   pl.BlockSpec((None, None, TK, d), lambda b, g, c: (b, g, c, 0)),
                   pl.BlockSpec((None, None, TK // TQ, d, TQ), lambda b, g, c: (b, g, c, 0, 0))),
        compiler_params=_params(("parallel", "parallel", "parallel")),
        name="nsa_prep",
    )(y3, y3, y3, y3)


def _nsa_attn_kernel(q_ref, gt_ref, kc_ref, vct_ref, kaug_ref, vst_ref, kwb_ref, vwt_ref, ovt_ref,
                     o_ref, acc_ref, *, seq):
    TQ, TK, d = NSA_TQ, NSA_TK, NSA_DH
    G = NSA_KV
    R = NSA_HEADS // G
    NQ = R * TQ
    groups = range(G)
    i = pl.program_id(1)
    t0 = i * TQ
    scale = d ** -0.5
    tcol = t0 + (lax.broadcasted_iota(jnp.int32, (1, NQ), 1) & (TQ - 1))

    q = q_ref[...]
    qf = [jnp.concatenate([q[:, (g * R + r) * d:(g * R + r + 1) * d] for r in range(R)], axis=0)
          for g in groups]
    qs = [x.astype(BF16) for x in qf]
    qs2 = [(x * (scale * LOG2E)).astype(BF16) for x in qf]

    ncp = kc_ref.shape[0]
    nid = lax.broadcasted_iota(jnp.int32, (ncp, NQ), 0)
    cmp_ok = nid * NSA_CMP_STRIDE + (NSA_CMP_LEN - 1) <= tcol
    s1 = [lax.dot_general(kc_ref[:, g * d:(g + 1) * d].astype(BF16), qs[g], NT_DIMS,
                          preferred_element_type=F32) * scale for g in groups]
    p1 = []
    for g in groups:
        s = jnp.where(cmp_ok, s1[g], -jnp.inf)
        m1 = jnp.max(s, axis=0, keepdims=True)
        m1 = jnp.where(m1 > -jnp.inf, m1, 0.0)
        e1 = jnp.exp(s - m1)
        z1 = jnp.sum(e1, axis=0, keepdims=True)
        p1.append(e1 / jnp.where(z1 > 0, z1, 1.0))
    o_cmp = [jnp.dot(vct_ref[g * d:(g + 1) * d, :].astype(BF16), p1[g].astype(BF16),
                     preferred_element_type=F32) for g in groups]

    W = NSA_WINDOW
    span = min(W + TQ, seq)
    kstart = pl.multiple_of(jnp.maximum(t0 + TQ - span, 0), TQ)
    kb = kstart // TQ
    rel = tcol - kstart - lax.broadcasted_iota(jnp.int32, (span, NQ), 0)
    win_ok = (rel >= 0) & (rel < W)
    s3 = [lax.dot_general(kwb_ref[g, pl.ds(kstart, span), :], qs2[g], NT_DIMS,
                          preferred_element_type=F32) for g in groups]
    o_win = []
    for g in groups:
        s = jnp.where(win_ok, s3[g], -jnp.inf)
        e3 = jnp.exp2(s - jnp.max(s, axis=0, keepdims=True))
        z3 = jnp.sum(e3, axis=0, keepdims=True)
        vwt = jnp.concatenate([vwt_ref[g, kb + h] for h in range(span // TQ)], axis=1)
        o_win.append(jnp.dot(vwt, e3.astype(BF16), preferred_element_type=F32) / z3)

    nblk = seq // NSA_SEL_LEN
    blk = lax.broadcasted_iota(jnp.int32, (LANES, TQ), 0)
    cur = (t0 + lax.broadcasted_iota(jnp.int32, (LANES, TQ), 1)) >> SEL_SHIFT
    forced = (blk == 0) | (blk == cur) | (blk == cur - 1)
    imp = []
    for g in groups:
        p1sum = p1[g][:, 0:TQ]
        for r in range(1, R):
            p1sum = p1sum + p1[g][:, r * TQ:(r + 1) * TQ]
        v = _dot_exact01(ovt_ref[...], p1sum)
        v = jnp.where(blk > cur, SEL_FUTURE, v)
        v = jnp.where(forced, SEL_FORCED, v)
        imp.append(jnp.where(blk >= nblk, SEL_TAKEN, v))
    sel = [jnp.zeros((LANES, TQ), F32) for _ in groups]
    for _ in range(min(NSA_N_SEL, nblk)):
        for g in groups:
            mx = jnp.max(imp[g], axis=0, keepdims=True)
            first = jnp.min(jnp.where(imp[g] == mx, blk, LANES), axis=0, keepdims=True)
            hit = blk == first
            sel[g] = jnp.where(hit, 1.0, sel[g])
            imp[g] = jnp.where(hit, SEL_TAKEN, imp[g])
    q_aug = []
    for g in groups:
        bias = jnp.where(sel[g] > 0, 0.0, SEL_MASK).T.astype(BF16)
        q_aug.append(jnp.concatenate([qs2[g], jnp.concatenate([bias] * R, axis=0)], axis=1))

    HC = NQ // 2
    krow = lax.broadcasted_iota(jnp.int32, (TK, HC), 0)
    acc_ref[...] = jnp.zeros_like(acc_ref)

    chains = [(g, slice(hh * HC, (hh + 1) * HC)) for g in groups for hh in range(2)]

    def sel_tiles(js, ml, masked):
        s = [[lax.dot_general(kaug_ref[g, j], q_aug[g][cols], NT_DIMS, preferred_element_type=F32)
              for g, cols in chains] for j in js]
        ml = list(ml)
        for t, j in enumerate(js):
            for c, (g, cols) in enumerate(chains):
                m_old, l_old = ml[2 * c], ml[2 * c + 1]
                sc = jnp.where(j * TK + krow <= tcol[:, cols], s[t][c], -jnp.inf) if masked else s[t][c]
                m_new = jnp.maximum(m_old, jnp.max(sc, axis=0, keepdims=True))
                alpha = jnp.exp2(m_old - m_new)
                p = jnp.exp2(sc - m_new)
                ml[2 * c] = m_new
                ml[2 * c + 1] = alpha * l_old + jnp.sum(p, axis=0, keepdims=True)
                acc_ref[c] = alpha * acc_ref[c] + jnp.dot(vst_ref[g, j], p.astype(BF16),
                                                          preferred_element_type=F32)
        return tuple(ml)

    n_full = (t0 + TQ + TK - 1) // TK - 1
    ml0 = (jnp.full((1, HC), -jnp.inf, F32), jnp.zeros((1, HC), F32)) * len(chains)
    ml = lax.fori_loop(0, n_full // 2, lambda u, c: sel_tiles((2 * u, 2 * u + 1), c, False), ml0)
    ml = lax.cond(n_full % 2 == 1, lambda c: sel_tiles((n_full - 1,), c, False), lambda c: c, ml)
    ml = sel_tiles((n_full,), ml, True)
    o_slc = [jnp.concatenate([acc_ref[2 * g] / ml[4 * g + 1], acc_ref[2 * g + 1] / ml[4 * g + 3]], axis=1)
             for g in groups]

    gt = _sigmoid(gt_ref[...]).T
    for g in groups:
        for r in range(R):
            c = 3 * (g * R + r)
            cols = slice(r * TQ, (r + 1) * TQ)
            o = (gt[c:c + 1] * o_cmp[g][:, cols] + gt[c + 1:c + 2] * o_slc[g][:, cols]
                 + gt[c + 2:c + 3] * o_win[g][:, cols])
            o_ref[:, (g * R + r) * d:(g * R + r + 1) * d] = o.T.astype(o_ref.dtype)


def nsa_attention(y3, k_cmp, v_cmp):
    batch, seq, n = y3.shape
    m = batch * seq
    TQ, TK, d = NSA_TQ, NSA_TK, NSA_DH
    R = NSA_HEADS // NSA_KV
    nq = seq // TQ
    ncp = k_cmp.shape[1]
    assert seq // NSA_SEL_LEN <= LANES and seq % TK == 0
    y2 = y3.reshape(m, n)
    v_cmp_t = v_cmp.transpose(0, 2, 1)
    nc = (seq - NSA_CMP_LEN) // NSA_CMP_STRIDE + 1
    cs = jnp.arange(ncp) * NSA_CMP_STRIDE
    ss = jnp.arange(LANES) * NSA_SEL_LEN
    overlap_t = ((cs[None, :] < ss[:, None] + NSA_SEL_LEN) & (cs[None, :] + NSA_CMP_LEN > ss[:, None])
                 & (jnp.arange(ncp)[None, :] < nc) & (jnp.arange(LANES)[:, None] < seq // NSA_SEL_LEN))
    overlap_t = overlap_t.astype(BF16)
    G = NSA_KV
    nk = seq // TK
    kaug, vst, kwb, vwt = nsa_prep(y3)
    whole = lambda shape: pl.BlockSpec((None,) + shape, lambda b, i: (b,) + (0,) * len(shape))
    return pl.pallas_call(
        functools.partial(_nsa_attn_kernel, seq=seq),
        out_shape=jax.ShapeDtypeStruct((m, MIX_HALF), BF16),
        grid=(batch, nq),
        in_specs=[
            pl.BlockSpec((TQ, NSA_HEADS * d), lambda b, i: (b * nq + i, 0)),
            pl.BlockSpec((TQ, LANES), lambda b, i: (b * nq + i, CD_GATE_COL // LANES)),
            whole((ncp, G * d)), whole((G * d, ncp)),
            whole((G, nk, TK, d + LANES)), whole((G, nk, d, TK)),
            whole((G, seq, d)), whole((G, seq // TQ, d, TQ)),
            pl.BlockSpec((LANES, ncp), lambda b, i: (0, 0)),
        ],
        out_specs=pl.BlockSpec((TQ, NSA_HEADS * d), lambda b, i: (b * nq + i, 0)),
        scratch_shapes=[pltpu.VMEM((2 * G, d, R * TQ // 2), F32)],
        compiler_params=_params(("parallel", "arbitrary")),
        name="nsa_attention",
    )(y2, y2, k_cmp, v_cmp_t, kaug, vst, kwb, vwt, overlap_t)


def _router_kernel(x_ref, nw_ref, w2_ref, wh_ref, bias_ref, o_ref):
    x = x_ref[...]
    xn = x * lax.rsqrt(jnp.mean(x * x, axis=-1, keepdims=True) + RMS_EPS) * nw_ref[...]
    xh = xn.astype(BF16)
    xl = (xn - xh.astype(F32)).astype(BF16)
    r1 = jnp.dot(xh, w2_ref[...], preferred_element_type=F32)
    r2 = jnp.dot(xl, wh_ref[...], preferred_element_type=F32)
    logits = r1[:, 0:LANES] + r1[:, LANES:] + r2 + bias_ref[...]
    lane = lax.broadcasted_iota(jnp.int32, logits.shape, 1)
    first_of = lambda hit: jnp.min(jnp.where(hit, lane, LANES), axis=-1, keepdims=True)
    gl = jnp.where(lane < MOE_GROUPS, logits, -jnp.inf)
    gmax = jnp.max(gl, axis=-1, keepdims=True)
    p_grp = 1.0 / jnp.sum(jnp.exp(gl - gmax), axis=-1, keepdims=True)
    g_idx = first_of(gl == gmax)
    e_lane = lane - MOE_GROUPS
    in_g = (e_lane >= 0) & (e_lane < MOE_EXPERTS) & ((e_lane >> 3) == g_idx)
    wl = jnp.where(in_g, logits, -jnp.inf)
    we = jnp.exp(wl - jnp.max(wl, axis=-1, keepdims=True))
    prob = jnp.where(in_g, we / jnp.sum(we, axis=-1, keepdims=True), -1.0)
    p0 = jnp.max(prob, axis=-1, keepdims=True)
    i0 = first_of(prob == p0)
    prob1 = jnp.where(lane == i0, -1.0, prob)
    p1 = jnp.max(prob1, axis=-1, keepdims=True)
    i1 = first_of(prob1 == p1)
    denom = p0 + p1
    vals = [(i0 - MOE_GROUPS).astype(F32), (i1 - MOE_GROUPS).astype(F32),
            p_grp * p0 / denom, p_grp * p1 / denom]
    out = jnp.zeros_like(logits)
    for c, v in enumerate(vals):
        out = jnp.where(lane == c, v, out)
    o_ref[...] = out


def router(h, nw, w_grp, b_grp, w_exp, b_exp, tm):
    m, d = h.shape
    assert MOE_EPG == 8
    wr = jnp.zeros((d, LANES), F32).at[:, 0:MOE_GROUPS].set(w_grp)
    wr = wr.at[:, MOE_GROUPS:MOE_GROUPS + MOE_EXPERTS].set(w_exp)
    bias = jnp.zeros((1, LANES), F32).at[0, 0:MOE_GROUPS].set(b_grp.astype(F32))
    bias = bias.at[0, MOE_GROUPS:MOE_GROUPS + MOE_EXPERTS].set(b_exp.astype(F32))
    wh = wr.astype(BF16)
    wl = (wr - wh.astype(F32)).astype(BF16)
    routed = pl.pallas_call(
        _router_kernel,
        out_shape=jax.ShapeDtypeStruct((m, LANES), F32),
        grid=(m // tm,),
        in_specs=[
            pl.BlockSpec((tm, d), lambda i: (i, 0)),
            pl.BlockSpec((1, d), lambda i: (0, 0)),
            pl.BlockSpec((d, 2 * LANES), lambda i: (0, 0)),
            pl.BlockSpec((d, LANES), lambda i: (0, 0)),
            pl.BlockSpec((1, LANES), lambda i: (0, 0)),
        ],
        out_specs=pl.BlockSpec((tm, LANES), lambda i: (i, 0)),
        compiler_params=_params(("parallel",)),
        name="router",
    )(h, nw.reshape(1, d), jnp.concatenate([wh, wl], axis=1), wh, bias)
    return routed[:, 0:MOE_TOPK].astype(jnp.int32), routed[:, MOE_TOPK:2 * MOE_TOPK]


def _cast3_kernel(a_ref, b_ref, c_ref, oa_ref, ob_ref, oc_ref):
    oa_ref[...] = a_ref[...].astype(oa_ref.dtype)
    ob_ref[...] = b_ref[...].astype(ob_ref.dtype)
    oc_ref[...] = c_ref[...].astype(oc_ref.dtype)


def expert_weights_bf16(w_gate, w_up, w_down, layer):
    _, E, d, dff = w_gate.shape
    split = 2
    in_spec = lambda r, c: pl.BlockSpec((None, None, r // split, c), lambda e, s: (layer, e, s, 0))
    out_spec = lambda r, c: pl.BlockSpec((None, r // split, c), lambda e, s: (e, s, 0))
    return pl.pallas_call(
        _cast3_kernel,
        out_shape=(jax.ShapeDtypeStruct((E, d, dff), BF16), jax.ShapeDtypeStruct((E, d, dff), BF16),
                   jax.ShapeDtypeStruct((E, dff, d), BF16)),
        grid=(E, split),
        in_specs=[in_spec(d, dff), in_spec(d, dff), in_spec(dff, d)],
        out_specs=(out_spec(d, dff), out_spec(d, dff), out_spec(dff, d)),
        compiler_params=_params(("parallel", "parallel")),
        name="expert_weights_bf16",
    )(w_gate, w_up, w_down)


SUBLANES = 8


def _gather_rows(idx_ref, base, src_ref, dst_ref, sem):
    def body(it, carry):
        for u in range(SUBLANES):
            tok = idx_ref[base + it * SUBLANES + u]
            pltpu.make_async_copy(src_ref.at[pl.ds(tok, 1)], dst_ref.at[it, pl.ds(u, 1)], sem).start()
        return carry
    lax.fori_loop(0, dst_ref.shape[0], body, 0)


def _wait_rows(dst_ref, sem):
    pltpu.make_async_copy(dst_ref, dst_ref, sem).wait()


def _expert_kernel(be_ref, nused_ref, tok_ref, h_ref, nw_ref, wg_ref, wu_ref, wd_ref, o_ref,
                   xbuf, sems):
    TM = MOE_TM
    i = pl.program_id(0)
    n_used = nused_ref[0]
    slot = i % 2

    @pl.when(i == 0)
    def _():
        _gather_rows(tok_ref, 0, h_ref, xbuf.at[0], sems.at[0])

    @pl.when(i + 1 < n_used)
    def _():
        _gather_rows(tok_ref, (i + 1) * TM, h_ref, xbuf.at[1 - slot], sems.at[1 - slot])

    @pl.when(i < n_used)
    def _():
        _wait_rows(xbuf.at[slot], sems.at[slot])
        x = xbuf[slot].reshape(TM, xbuf.shape[-1])
        xn = (x * lax.rsqrt(jnp.mean(x * x, axis=-1, keepdims=True) + RMS_EPS) * nw_ref[...]).astype(BF16)
        hg = jnp.dot(xn, wg_ref[...], preferred_element_type=F32)
        hu = jnp.dot(xn, wu_ref[...], preferred_element_type=F32)
        hb = (_silu(hg) * hu).astype(BF16)
        o_ref[...] = jnp.dot(hb, wd_ref[...], preferred_element_type=F32)

    @pl.when(i >= n_used)
    def _():
        o_ref[...] = jnp.zeros_like(o_ref)


def expert_ffn(h, nw, row_tok, blk_expert, n_used, w_gate, w_up, w_down):
    m, d = h.shape
    rows = row_tok.shape[0]
    TM = MOE_TM
    nb = rows // TM
    dff = w_gate.shape[2]
    grid_spec = pltpu.PrefetchScalarGridSpec(
        num_scalar_prefetch=3,
        grid=(nb,),
        in_specs=[
            pl.BlockSpec(memory_space=pl.ANY),
            pl.BlockSpec((1, d), lambda i, be, nu, tok: (0, 0)),
            pl.BlockSpec((None, d, dff), lambda i, be, nu, tok: (be[i], 0, 0)),
            pl.BlockSpec((None, d, dff), lambda i, be, nu, tok: (be[i], 0, 0)),
            pl.BlockSpec((None, dff, d), lambda i, be, nu, tok: (be[i], 0, 0)),
        ],
        out_specs=pl.BlockSpec((TM, d), lambda i, be, nu, tok: (i, 0)),
        scratch_shapes=[pltpu.VMEM((2, TM // SUBLANES, SUBLANES, d), F32), pltpu.SemaphoreType.DMA((2,))],
    )
    return pl.pallas_call(
        _expert_kernel,
        out_shape=jax.ShapeDtypeStruct((rows, d), F32),
        grid_spec=grid_spec,
        compiler_params=_params(("arbitrary",)),
        name="expert_ffn",
    )(blk_expert, n_used, row_tok, h, nw.reshape(1, d), w_gate, w_up, w_down)


def _combine_kernel(dest_ref, ys_ref, h_ref, gate_ref, fw_ref, o_ref, ybuf, sems, *, final_norm):
    tc = h_ref.shape[0]
    i = pl.program_id(0)
    n = pl.num_programs(0)
    slot = i % 2

    @pl.when(i == 0)
    def _():
        _gather_rows(dest_ref, 0, ys_ref, ybuf.at[0], sems.at[0])

    @pl.when(i + 1 < n)
    def _():
        _gather_rows(dest_ref, (i + 1) * 2 * tc, ys_ref, ybuf.at[1 - slot], sems.at[1 - slot])

    _wait_rows(ybuf.at[slot], sems.at[slot])
    gate = gate_ref[...]
    d = h_ref.shape[1]
    g8 = tc // SUBLANES
    y0 = ybuf[slot, 0:g8].reshape(tc, d)
    y1 = ybuf[slot, g8:].reshape(tc, d)
    out = h_ref[...] + (y0 * gate[:, 0:1] + y1 * gate[:, 1:2])
    if final_norm:
        out = out * lax.rsqrt(jnp.mean(out * out, axis=-1, keepdims=True) + RMS_EPS) * fw_ref[...]
    o_ref[...] = out


def moe_combine(h, ys, dest_tiles, gate, final_w, tc, final_norm):
    m, d = h.shape
    grid_spec = pltpu.PrefetchScalarGridSpec(
        num_scalar_prefetch=1,
        grid=(m // tc,),
        in_specs=[
            pl.BlockSpec(memory_space=pl.ANY),
            pl.BlockSpec((tc, d), lambda i, dst: (i, 0)),
            pl.BlockSpec((tc, MOE_TOPK), lambda i, dst: (i, 0)),
            pl.BlockSpec((1, d), lambda i, dst: (0, 0)),
        ],
        out_specs=pl.BlockSpec((tc, d), lambda i, dst: (i, 0)),
        scratch_shapes=[pltpu.VMEM((2, 2 * tc // SUBLANES, SUBLANES, d), F32),
                        pltpu.SemaphoreType.DMA((2,))],
    )
    return pl.pallas_call(
        functools.partial(_combine_kernel, final_norm=final_norm),
        out_shape=jax.ShapeDtypeStruct((m, d), F32),
        grid_spec=grid_spec,
        compiler_params=_params(("arbitrary",)),
        name="moe_combine",
    )(dest_tiles, ys, h, gate, final_w.reshape(1, d))


def _dispatch_tables(expert):
    n = expert.shape[0]
    K, E, TM = MOE_TOPK, MOE_EXPERTS, MOE_TM
    A = n * K
    flat_e = expert.reshape(A)
    iota = jnp.arange(A, dtype=jnp.int32)
    e_sorted, order = lax.sort((flat_e, iota), num_keys=1)
    counts = jnp.sum((flat_e[:, None] == jnp.arange(E, dtype=jnp.int32)[None, :]).astype(jnp.int32), axis=0)
    padded = (counts + TM - 1) // TM * TM
    pad_end = jnp.cumsum(padded)
    pad_start = pad_end - padded
    start = jnp.cumsum(counts) - counts
    dest_sorted = (pad_start[e_sorted] + iota - start[e_sorted]).astype(jnp.int32)
    _, dest = lax.sort((order, dest_sorted), num_keys=1)
    rows = A + E * TM
    nb = rows // TM
    blk_expert = jnp.minimum(jnp.sum((jnp.arange(nb, dtype=jnp.int32)[:, None] * TM >= pad_end[None, :])
                                     .astype(jnp.int32), axis=1), E - 1).astype(jnp.int32)
    r = jnp.arange(rows, dtype=jnp.int32)
    e_row = jnp.repeat(blk_expert, TM)
    pos = r - pad_start[e_row].astype(jnp.int32)
    src = jnp.clip(start[e_row].astype(jnp.int32) + pos, 0, A - 1)
    row_tok = jnp.where(pos < counts[e_row], order[src] // K, 0).astype(jnp.int32)
    n_used = (pad_end[-1] // TM).astype(jnp.int32).reshape(1)
    return dest.reshape(n, K), row_tok, blk_expert, n_used


def hier_moe(h, nw, w_grp, b_grp, w_exp, b_exp, w_gate_all, w_up_all, w_down_all, layer, final_w,
             final_norm):
    m, d = h.shape
    tc = 256
    expert, gate = router(h, nw, w_grp, b_grp, w_exp, b_exp, 512)
    dest, row_tok, blk_expert, n_used = _dispatch_tables(expert)
    w_gate, w_up, w_down = expert_weights_bf16(w_gate_all, w_up_all, w_down_all, layer)
    ys = expert_ffn(h, nw, row_tok, blk_expert, n_used, w_gate, w_up, w_down)
    dest_tiles = dest.reshape(m // tc, tc, MOE_TOPK).transpose(0, 2, 1).reshape(-1)
    return moe_combine(h, ys, dest_tiles, gate, final_w, tc, final_norm)


def _cd_in_weights(w):
    gates = w[:, 2560:2584]
    pad = jnp.zeros((w.shape[0], CD_N - CD_GATE_COL - gates.shape[1]), w.dtype)
    return jnp.concatenate([w[:, 0:2560], w[:, 2584:], gates, pad], axis=1)


def mixer_ab(h, nw, w_in, w_out, lb, hg_norm, sw_sinks, batch, seq):
    y = norm_matmul(h, nw, w_in.astype(BF16), min(1024, h.shape[0]), 1408)
    o_a = hgrn2(y, lb, hg_norm, batch, seq, min(512, seq))
    o_b = swa(y, sw_sinks, batch, seq)
    return outproj_residual(o_a, o_b, w_out.astype(BF16), h, 512)


def mixer_cd(h, nw, w_in, w_out, pe_k, w1_k, w2_k, pe_v, w1_v, w2_v, ret_norm, batch, seq):
    y = norm_matmul(h, nw, _cd_in_weights(w_in).astype(BF16), min(1024, h.shape[0]), 1152)
    y3 = y.reshape(batch, seq, CD_N)
    k_cmp, v_cmp = nsa_compress(y3, pe_k, w1_k, w2_k, pe_v, w1_v, w2_v)
    o_c = nsa_attention(y3, k_cmp, v_cmp)
    o_d = retention(y, ret_norm, batch, seq, min(512, seq))
    return outproj_residual(o_c, o_d, w_out.astype(BF16), h, 512)


def kernel(x, norm_mix, norm_ffn, norm_final, ab_w_in, ab_w_out, hg_lb_logits, hg_norm, sw_sinks, cd_w_in, cd_w_out, nsa_pe_k, nsa_w1_k, nsa_w2_k, nsa_pe_v, nsa_w1_v, nsa_w2_v, ret_norm, moe_w_grp, moe_b_grp, moe_w_exp, moe_b_exp, moe_w_gate, moe_w_up, moe_w_down):
    batch, seq, d = x.shape
    depth = norm_mix.shape[0]
    lb_all = jnp.cumsum(jax.nn.softmax(hg_lb_logits.astype(F32), axis=0), axis=0)
    h = x.reshape(batch * seq, d)
    for layer in range(depth):
        if layer % 2 == 0:
            e = layer // 2
            h = mixer_ab(h, norm_mix[layer], ab_w_in[e], ab_w_out[e], lb_all[layer], hg_norm[e],
                         sw_sinks[e], batch, seq)
        else:
            o = layer // 2
            h = mixer_cd(h, norm_mix[layer], cd_w_in[o], cd_w_out[o], nsa_pe_k[o], nsa_w1_k[o],
                         nsa_w2_k[o], nsa_pe_v[o], nsa_w1_v[o], nsa_w2_v[o], ret_norm[o], batch, seq)
        h = hier_moe(h, norm_ffn[layer], moe_w_grp[layer], moe_b_grp[layer], moe_w_exp[layer],
                     moe_b_exp[layer], moe_w_gate, moe_w_up, moe_w_down, layer,
                     norm_final, layer == depth - 1)
    return h.reshape(batch, seq, d)
```

```python
import functools

import jax
import jax.numpy as jnp
from jax import lax
from jax.experimental import pallas as pl
from jax.experimental.pallas import tpu as pltpu

F32 = jnp.float32
BF16 = jnp.bfloat16

D_MODEL = 2048
MIX_HALF = D_MODEL // 2
RMS_EPS = 1e-6
LANES = 128
VMEM_LIMIT = 56 * 1024 * 1024

HG_DK = 128
HG_HEADS = 8
HG_CHUNK = 64
HG_SUB = 16
SW_DH = 128
SW_HEADS = 8
SW_KV = 2
SW_WINDOW = 128
NSA_DH = 128
NSA_HEADS = 8
NSA_KV = 2
NSA_CMP_LEN = 32
NSA_CMP_STRIDE = 16
NSA_CMP_HIDDEN = 128
NSA_SEL_LEN = 64
NSA_N_SEL = 8
NSA_WINDOW = 512
RET_DK = 128
RET_DV = 256
RET_HEADS = 4
RET_CHUNK = 128
RET_THETA_BASE = 10000.0
MOE_GROUPS = 4
MOE_EPG = 8
MOE_EXPERTS = 32
MOE_TOPK = 2
MOE_DFF = 1024
MOE_TM = 256

LOG2E = 1.4426950408889634


def _offsets(sizes):
    out, acc = {}, 0
    for name, size in sizes:
        out[name] = acc
        acc += size
    return out, acc


AB_COL, AB_N = _offsets((("hq", HG_HEADS * HG_DK), ("hf", HG_HEADS * HG_DK), ("hi", HG_HEADS * HG_DK),
                         ("hg", HG_HEADS * HG_DK), ("sq", SW_HEADS * SW_DH), ("sk", SW_KV * SW_DH),
                         ("sv", SW_KV * SW_DH)))
CD_SRC, _ = _offsets((("nq", NSA_HEADS * NSA_DH),) + tuple((n, NSA_KV * NSA_DH) for n in
                     ("nkc", "nvc", "nks", "nvs", "nkw", "nvw")) + (("gt", NSA_HEADS * 3),
                     ("rq", RET_HEADS * RET_DK), ("rk", RET_HEADS * RET_DK), ("rv", RET_HEADS * RET_DV),
                     ("rg", RET_HEADS * RET_DV)))
CD_COL, CD_N = _offsets((("nq", NSA_HEADS * NSA_DH),) + tuple((n, NSA_KV * NSA_DH) for n in
                        ("nkc", "nvc", "nks", "nvs", "nkw", "nvw")) + (("rq", RET_HEADS * RET_DK),
                        ("rk", RET_HEADS * RET_DK), ("rv", RET_HEADS * RET_DV), ("rg", RET_HEADS * RET_DV),
                        ("gt", LANES)))
CD_GATE_COL = CD_COL["gt"]

NT_DIMS = (((1,), (1,)), ((), ()))
TN_DIMS = (((0,), (0,)), ((), ()))


def _params(sem, vmem=VMEM_LIMIT):
    return pltpu.CompilerParams(dimension_semantics=sem, vmem_limit_bytes=vmem)


def _sigmoid(x):
    return 1.0 / (1.0 + jnp.exp(-x))


def _silu(x):
    return x * _sigmoid(x)


def _dot_exact01(a01, x):
    n = x.shape[1]
    hi = x.astype(BF16)
    rem = x - hi.astype(F32)
    mid = rem.astype(BF16)
    lo = (rem - mid.astype(F32)).astype(BF16)
    r = jnp.dot(a01, jnp.concatenate([hi, mid, lo], axis=1), preferred_element_type=F32)
    return r[:, 0:n] + r[:, n:2 * n] + r[:, 2 * n:]


def _norm_matmul_kernel(x_ref, nw_ref, w_ref, o_ref, xn_ref):
    @pl.when(pl.program_id(1) == 0)
    def _():
        x = x_ref[...]
        ms = jnp.mean(x * x, axis=-1, keepdims=True)
        xn_ref[...] = (x * lax.rsqrt(ms + RMS_EPS) * nw_ref[...]).astype(BF16)

    o_ref[...] = jnp.dot(xn_ref[...], w_ref[...], preferred_element_type=F32)


def norm_matmul(x, nw, w, tm, tn):
    m, d = x.shape
    n = w.shape[1]
    return pl.pallas_call(
        _norm_matmul_kernel,
        out_shape=jax.ShapeDtypeStruct((m, n), F32),
        grid=(m // tm, n // tn),
        in_specs=[
            pl.BlockSpec((tm, d), lambda i, j: (i, 0)),
            pl.BlockSpec((1, d), lambda i, j: (0, 0)),
            pl.BlockSpec((d, tn), lambda i, j: (0, j)),
        ],
        out_specs=pl.BlockSpec((tm, tn), lambda i, j: (i, j)),
        scratch_shapes=[pltpu.VMEM((tm, d), BF16)],
        compiler_params=_params(("parallel", "arbitrary")),
        name="norm_matmul",
    )(x, nw.reshape(1, d), w)


def _outproj_kernel(a_ref, b_ref, w_ref, h_ref, nw_ref, w2_ref, wh_ref, bias_ref, o_ref, r_ref):
    half = a_ref.shape[1]
    acc = jnp.dot(a_ref[...], w_ref[0:half, :], preferred_element_type=F32)
    acc = acc + jnp.dot(b_ref[...], w_ref[half:, :], preferred_element_type=F32)
    out = h_ref[...] + acc
    o_ref[...] = out
    r_ref[...] = _route_rows(out, nw_ref[...], w2_ref[...], wh_ref[...], bias_ref[...])


def outproj_residual(a, b, w, h, router_ops, tm):
    m, half = a.shape
    d = w.shape[1]
    const = lambda x: pl.BlockSpec(x.shape, lambda i: (0, 0))
    return pl.pallas_call(
        _outproj_kernel,
        out_shape=(jax.ShapeDtypeStruct((m, d), F32), jax.ShapeDtypeStruct((m, LANES), F32)),
        grid=(m // tm,),
        in_specs=[
            pl.BlockSpec((tm, half), lambda i: (i, 0)),
            pl.BlockSpec((tm, half), lambda i: (i, 0)),
            pl.BlockSpec((2 * half, d), lambda i: (0, 0)),
            pl.BlockSpec((tm, d), lambda i: (i, 0)),
        ] + [const(x) for x in router_ops],
        out_specs=(pl.BlockSpec((tm, d), lambda i: (i, 0)), pl.BlockSpec((tm, LANES), lambda i: (i, 0))),
        compiler_params=_params(("parallel",)),
        name="outproj_residual",
    )(a, b, w, h, *router_ops)


def _hgrn2_kernel(q_ref, f_ref, i_ref, g_ref, lb_ref, nw_ref, o_ref, st_ref, *, n_chunks):
    C, c = HG_CHUNK, HG_SUB
    nsub = C // c

    @pl.when(pl.program_id(2) == 0)
    def _():
        st_ref[...] = jnp.zeros_like(st_ref)

    lb = lb_ref[...]
    nw = nw_ref[...]
    tri = (lax.broadcasted_iota(jnp.int32, (C, C), 0)
           >= lax.broadcasted_iota(jnp.int32, (C, C), 1)).astype(BF16)
    hs = c // 2
    row8 = lax.broadcasted_iota(jnp.int32, (1, hs, 1), 1)
    chunks = range(n_chunks)
    rows_of = lambda ci: slice(ci * C, (ci + 1) * C)

    q = [q_ref[rows_of(ci), :] for ci in chunks]
    v = [i_ref[rows_of(ci), :] for ci in chunks]
    vb = [x.astype(BF16) for x in v]
    f = [lb + (1.0 - lb) * _sigmoid(f_ref[rows_of(ci), :]) for ci in chunks]
    k = [1.0 - x for x in f]
    b = [_dot_exact01(tri, jnp.log(x)) for x in f]

    a_off = []
    for ci in chunks:
        for i in range(1, nsub):
            r = b[ci][i * c - 1:i * c, :]
            qi = q[ci][i * c:(i + 1) * c] * jnp.exp(b[ci][i * c:(i + 1) * c] - r)
            ki = k[ci][0:i * c] * jnp.exp(r - b[ci][0:i * c])
            a_off.append(lax.dot_general(qi.astype(BF16), ki.astype(BF16), NT_DIMS,
                                         preferred_element_type=F32))
    o_off = []
    for ci in chunks:
        rows = [jnp.zeros((c, HG_DK), F32)]
        for i in range(1, nsub):
            a = a_off[ci * (nsub - 1) + i - 1]
            rows.append(jnp.dot(a.astype(BF16), vb[ci][0:i * c], preferred_element_type=F32))
        o_off.append(jnp.concatenate(rows, axis=0))
    inc = []
    for ci in chunks:
        bend = b[ci][C - 1:C, :]
        kd = k[ci] * jnp.exp(bend - b[ci])
        inc.append(lax.dot_general(vb[ci], kd.astype(BF16), TN_DIMS, preferred_element_type=F32))
    st = st_ref[...]
    o_st = []
    for ci in chunks:
        o_st.append(lax.dot_general((q[ci] * jnp.exp(b[ci])).astype(BF16), st.astype(BF16), NT_DIMS,
                                    preferred_element_type=F32))
        st = st * jnp.exp(b[ci][C - 1:C, :]) + inc[ci]
    st_ref[...] = st

    for ci in chunks:
        b4 = b[ci].reshape(nsub, c, HG_DK)
        q4 = q[ci].reshape(nsub, c, HG_DK)
        k4 = k[ci].reshape(nsub, c, HG_DK)
        v4 = v[ci].reshape(nsub, c, HG_DK)
        bt, bb = b4[:, 0:hs], b4[:, hs:]
        qt, qb = q4[:, 0:hs], q4[:, hs:]
        od_t = jnp.zeros((nsub, hs, HG_DK), F32)
        od_b = jnp.zeros((nsub, hs, HG_DK), F32)
        for s in range(c):
            piv, kp, vp = b4[:, s:s + 1], k4[:, s:s + 1], v4[:, s:s + 1]
            if s < hs:
                e = jnp.exp(jnp.minimum(bt - piv, 0.0))
                a = jnp.sum(qt * e * kp, axis=-1, keepdims=True)
                od_t = od_t + jnp.where(row8 >= s, a, 0.0) * vp
                e = jnp.exp(jnp.minimum(bb - piv, 0.0))
                od_b = od_b + jnp.sum(qb * e * kp, axis=-1, keepdims=True) * vp
            else:
                e = jnp.exp(jnp.minimum(bb - piv, 0.0))
                a = jnp.sum(qb * e * kp, axis=-1, keepdims=True)
                od_b = od_b + jnp.where(row8 >= s - hs, a, 0.0) * vp
        o = o_off[ci] + o_st[ci] + jnp.concatenate([od_t, od_b], axis=1).reshape(C, HG_DK)
        y = o * lax.rsqrt(jnp.mean(o * o, axis=-1, keepdims=True) + RMS_EPS) * nw
        o_ref[rows_of(ci), :] = (y * _silu(g_ref[rows_of(ci), :])).astype(o_ref.dtype)


def hgrn2(y, lb, nw, batch, seq, tc):
    m = y.shape[0]
    nt = seq // tc
    row = lambda b, h, t: b * nt + t
    hd = HG_HEADS
    return pl.pallas_call(
        functools.partial(_hgrn2_kernel, n_chunks=tc // HG_CHUNK),
        out_shape=jax.ShapeDtypeStruct((m, MIX_HALF), BF16),
        grid=(batch, hd, nt),
        in_specs=[
            pl.BlockSpec((tc, HG_DK), lambda b, h, t: (row(b, h, t), h)),
            pl.BlockSpec((tc, HG_DK), lambda b, h, t: (row(b, h, t), hd + h)),
            pl.BlockSpec((tc, HG_DK), lambda b, h, t: (row(b, h, t), 2 * hd + h)),
            pl.BlockSpec((tc, HG_DK), lambda b, h, t: (row(b, h, t), 3 * hd + h)),
            pl.BlockSpec((None, 1, HG_DK), lambda b, h, t: (h, 0, 0)),
            pl.BlockSpec((None, 1, HG_DK), lambda b, h, t: (h, 0, 0)),
        ],
        out_specs=pl.BlockSpec((tc, HG_DK), lambda b, h, t: (row(b, h, t), h)),
        scratch_shapes=[pltpu.VMEM((HG_DK, HG_DK), F32)],
        compiler_params=_params(("parallel", "parallel", "arbitrary")),
        name="hgrn2",
    )(y, y, y, y, lb.reshape(hd, 1, HG_DK), nw.reshape(hd, 1, HG_DK))


def _swa_kernel(q_ref, kp_ref, kc_ref, vp_ref, vc_ref, sink_ref, o_ref):
    W, d = SW_WINDOW, SW_DH
    G = SW_KV
    R = SW_HEADS // G
    NQ = R * W
    groups = range(G)
    n = pl.program_id(1)
    q = q_ref[...]
    qs = [(jnp.concatenate([q[:, (g * R + r) * d:(g * R + r + 1) * d] for r in range(R)], axis=0)
           * (d ** -0.5 * LOG2E)).astype(BF16) for g in groups]
    kk = [jnp.concatenate([kp_ref[:, g * d:(g + 1) * d], kc_ref[:, g * d:(g + 1) * d]],
                          axis=0).astype(BF16) for g in groups]
    vt = [jnp.concatenate([vp_ref[:, g * d:(g + 1) * d].T, vc_ref[:, g * d:(g + 1) * d].T],
                          axis=1).astype(BF16) for g in groups]
    s = [lax.dot_general(kk[g], qs[g], NT_DIMS, preferred_element_type=F32) for g in groups]
    kj = lax.broadcasted_iota(jnp.int32, (2 * W, NQ), 0)
    rel = (lax.broadcasted_iota(jnp.int32, (2 * W, NQ), 1) & (W - 1)) + W - kj
    valid = (rel >= 0) & (rel < W) & ((kj >= W) | (n > 0))
    outs = []
    for g in groups:
        sg = jnp.where(valid, s[g], -jnp.inf)
        sink = sink_ref[g] * LOG2E
        mx = jnp.maximum(jnp.max(sg, axis=0, keepdims=True), sink)
        p = jnp.exp2(sg - mx)
        z = jnp.sum(p, axis=0, keepdims=True) + jnp.exp2(sink - mx)
        outs.append(jnp.dot(vt[g], p.astype(BF16), preferred_element_type=F32) / z)
    for g in groups:
        for r in range(R):
            o_ref[:, (g * R + r) * d:(g * R + r + 1) * d] = outs[g][:, r * W:(r + 1) * W].T.astype(o_ref.dtype)


def swa(y, sinks, batch, seq):
    m = y.shape[0]
    W, d, G = SW_WINDOW, SW_DH, SW_KV
    R = SW_HEADS // G
    nb = seq // W
    qcol = AB_COL["sq"] // (SW_HEADS * d)
    kcol = AB_COL["sk"] // (G * d)
    vcol = AB_COL["sv"] // (G * d)
    cur = lambda b, n: b * nb + n
    prev = lambda b, n: b * nb + jnp.maximum(n - 1, 0)
    sink_cols = jnp.repeat(sinks.astype(F32).reshape(G, R), W, axis=1).reshape(G, 1, R * W)
    return pl.pallas_call(
        _swa_kernel,
        out_shape=jax.ShapeDtypeStruct((m, MIX_HALF), BF16),
        grid=(batch, nb),
        in_specs=[
            pl.BlockSpec((W, SW_HEADS * d), lambda b, n: (cur(b, n), qcol)),
            pl.BlockSpec((W, G * d), lambda b, n: (prev(b, n), kcol)),
            pl.BlockSpec((W, G * d), lambda b, n: (cur(b, n), kcol)),
            pl.BlockSpec((W, G * d), lambda b, n: (prev(b, n), vcol)),
            pl.BlockSpec((W, G * d), lambda b, n: (cur(b, n), vcol)),
            pl.BlockSpec((G, 1, R * W), lambda b, n: (0, 0, 0)),
        ],
        out_specs=pl.BlockSpec((W, SW_HEADS * d), lambda b, n: (cur(b, n), 0)),
        compiler_params=_params(("parallel", "parallel")),
        name="swa",
    )(y, y, y, y, y, sink_cols)


def _retention_kernel(q_ref, k_ref, v_ref, g_ref, cos_ref, sin_ref, dmat_ref, xi_ref, zeta_ref,
                      cdec_ref, nw_ref, o_ref, st_ref, *, n_chunks):
    C = RET_CHUNK

    @pl.when(pl.program_id(2) == 0)
    def _():
        st_ref[...] = jnp.zeros_like(st_ref)

    dmat = dmat_ref[...]
    xi = xi_ref[...]
    zeta = zeta_ref[...]
    cdec = cdec_ref[...]
    nw = nw_ref[...]

    chunks = range(n_chunks)
    rows_of = lambda ci: slice(ci * C, (ci + 1) * C)

    def rotate(x_ref, ci):
        x = x_ref[rows_of(ci), :]
        return x * cos_ref[rows_of(ci), :] + pltpu.roll(x, RET_DK // 2, 1) * sin_ref[rows_of(ci), :]

    qr = [rotate(q_ref, ci) for ci in chunks]
    kr = [rotate(k_ref, ci) * (RET_DK ** -0.5) for ci in chunks]
    vb = [v_ref[rows_of(ci), :].astype(BF16) for ci in chunks]
    inner = [lax.dot_general(qr[ci].astype(BF16), kr[ci].astype(BF16), NT_DIMS,
                             preferred_element_type=F32) * dmat for ci in chunks]
    inc = [lax.dot_general((kr[ci] * zeta).astype(BF16), vb[ci], TN_DIMS,
                           preferred_element_type=F32) for ci in chunks]
    o_in = [jnp.dot(inner[ci].astype(BF16), vb[ci], preferred_element_type=F32) for ci in chunks]
    st = st_ref[...]
    o_cross = []
    for ci in chunks:
        o_cross.append(jnp.dot((qr[ci] * xi).astype(BF16), st.astype(BF16), preferred_element_type=F32))
        st = cdec * st + inc[ci]
    st_ref[...] = st
    for ci in chunks:
        o = o_in[ci] + o_cross[ci]
        cen = o - jnp.mean(o, axis=-1, keepdims=True)
        y = cen * lax.rsqrt(jnp.mean(cen * cen, axis=-1, keepdims=True) + RMS_EPS) * nw
        o_ref[rows_of(ci), :] = (y * _silu(g_ref[rows_of(ci), :])).astype(o_ref.dtype)


def retention(y, nw, batch, seq, tc):
    m = y.shape[0]
    nt = seq // tc
    H, C = RET_HEADS, RET_CHUNK
    log_g = jnp.log1p(-jnp.exp2(-5.0 - jnp.arange(H, dtype=F32)))
    j = jnp.arange(C, dtype=F32)
    diff = j[:, None] - j[None, :]
    dmat = jnp.where(diff >= 0, jnp.exp(jnp.maximum(diff, 0.0)[None] * log_g[:, None, None]), 0.0)
    zeta = jnp.exp((C - 1 - j)[None, :] * log_g[:, None]).reshape(H, C, 1)
    xi = jnp.exp((j + 1.0)[None, :] * log_g[:, None]).reshape(H, C, 1)
    cdec = jnp.broadcast_to(jnp.exp(C * log_g)[:, None, None], (H, 1, RET_DV))
    half = RET_DK // 2
    theta = 1.0 / (RET_THETA_BASE ** jnp.linspace(0.0, 1.0, half, dtype=F32))
    ang = jnp.arange(seq).astype(F32)[:, None] * theta[None, :]
    cos2 = jnp.concatenate([jnp.cos(ang), jnp.cos(ang)], axis=-1)
    sin2 = jnp.concatenate([-jnp.sin(ang), jnp.sin(ang)], axis=-1)

    row = lambda b, h, t: b * nt + t
    qcol = CD_COL["rq"] // RET_DK
    kcol = CD_COL["rk"] // RET_DK
    vcol = CD_COL["rv"] // RET_DV
    gcol = CD_COL["rg"] // RET_DV
    return pl.pallas_call(
        functools.partial(_retention_kernel, n_chunks=tc // C),
        out_shape=jax.ShapeDtypeStruct((m, MIX_HALF), BF16),
        grid=(batch, H, nt),
        in_specs=[
            pl.BlockSpec((tc, RET_DK), lambda b, h, t: (row(b, h, t), qcol + h)),
            pl.BlockSpec((tc, RET_DK), lambda b, h, t: (row(b, h, t), kcol + h)),
            pl.BlockSpec((tc, RET_DV), lambda b, h, t: (row(b, h, t), vcol + h)),
            pl.BlockSpec((tc, RET_DV), lambda b, h, t: (row(b, h, t), gcol + h)),
            pl.BlockSpec((tc, RET_DK), lambda b, h, t: (t, 0)),
            pl.BlockSpec((tc, RET_DK), lambda b, h, t: (t, 0)),
            pl.BlockSpec((None, C, C), lambda b, h, t: (h, 0, 0)),
            pl.BlockSpec((None, C, 1), lambda b, h, t: (h, 0, 0)),
            pl.BlockSpec((None, C, 1), lambda b, h, t: (h, 0, 0)),
            pl.BlockSpec((None, 1, RET_DV), lambda b, h, t: (h, 0, 0)),
            pl.BlockSpec((None, 1, RET_DV), lambda b, h, t: (h, 0, 0)),
        ],
        out_specs=pl.BlockSpec((tc, RET_DV), lambda b, h, t: (row(b, h, t), h)),
        scratch_shapes=[pltpu.VMEM((RET_DK, RET_DV), F32)],
        compiler_params=_params(("parallel", "parallel", "arbitrary")),
        name="retention",
    )(y, y, y, y, cos2, sin2, dmat, xi, zeta, cdec, nw.reshape(H, 1, RET_DV))


def _gelu_tanh(x):
    return 0.5 * x * (1.0 + jnp.tanh(0.7978845608028654 * (x + 0.044715 * x * x * x)))


def _nsa_compress_kernel(xk_ref, xv_ref, pek_ref, pev_ref, w1k_ref, w1v_ref, w2k_ref, w2v_ref,
                         ok_ref, ov_ref):
    gh = NSA_KV * NSA_CMP_HIDDEN

    def one(x_ref, pe_ref, w1_ref, w2_ref, o_ref):
        w1 = w1_ref[...]
        pq = jnp.dot(x_ref[...].astype(BF16), w1, preferred_element_type=F32)
        pe = jnp.dot(pe_ref[...].astype(BF16), w1, preferred_element_type=F32)
        const = pe[0:1, 0:gh] + pe[1:2, gh:]
        nrow = pq.shape[0]
        nxt = pltpu.roll(pq[:, gh:], nrow - 1, 0)
        hid = _gelu_tanh(pq[:, 0:gh] + nxt + const)
        o_ref[...] = jnp.dot(hid.astype(BF16), w2_ref[...], preferred_element_type=F32)

    one(xk_ref, pek_ref, w1k_ref, w2k_ref, ok_ref)
    one(xv_ref, pev_ref, w1v_ref, w2v_ref, ov_ref)


def _compress_weights(pe, w1, w2):
    G, L, S = NSA_KV, NSA_CMP_LEN, NSA_CMP_STRIDE
    d, hd = NSA_DH, NSA_CMP_HIDDEN
    eye = jnp.eye(G, dtype=F32)

    def big(w):
        return jnp.einsum('ldh,ge->lgdeh', w, eye).reshape(S * G * d, G * hd)

    w1big = jnp.concatenate([big(w1[:S]), big(w1[S:])], axis=1).astype(BF16)
    w2big = jnp.einsum('hd,ge->ghed', w2, eye).reshape(G * hd, G * d).astype(BF16)

    def pe_row(p):
        return jnp.broadcast_to(p[:, None, :], (S, G, d)).reshape(S * G * d)

    pe8 = jnp.zeros((8, S * G * d), F32).at[0].set(pe_row(pe[:S])).at[1].set(pe_row(pe[S:]))
    return pe8, w1big, w2big


def nsa_compress(y3, pe_k, w1_k, w2_k, pe_v, w1_v, w2_v):
    batch, seq, _ = y3.shape
    G, S, d = NSA_KV, NSA_CMP_STRIDE, NSA_DH
    nrow = seq // S
    xk = y3[:, :, CD_COL["nkc"]:CD_COL["nkc"] + G * d].reshape(batch, nrow, S * G * d)
    xv = y3[:, :, CD_COL["nvc"]:CD_COL["nvc"] + G * d].reshape(batch, nrow, S * G * d)
    pek, w1k, w2k = _compress_weights(pe_k, w1_k, w2_k)
    pev, w1v, w2v = _compress_weights(pe_v, w1_v, w2_v)
    kdim = S * G * d
    full = lambda shape: pl.BlockSpec(shape, lambda b: (0,) * len(shape))
    out = jax.ShapeDtypeStruct((batch, nrow, G * d), F32)
    return pl.pallas_call(
        _nsa_compress_kernel,
        out_shape=(out, out),
        grid=(batch,),
        in_specs=[
            pl.BlockSpec((None, nrow, kdim), lambda b: (b, 0, 0)),
            pl.BlockSpec((None, nrow, kdim), lambda b: (b, 0, 0)),
            full((8, kdim)), full((8, kdim)),
            full(w1k.shape), full(w1v.shape), full(w2k.shape), full(w2v.shape),
        ],
        out_specs=(pl.BlockSpec((None, nrow, G * d), lambda b: (b, 0, 0)),
                   pl.BlockSpec((None, nrow, G * d), lambda b: (b, 0, 0))),
        compiler_params=_params(("parallel",)),
        name="nsa_compress",
    )(xk, xv, pek, pev, w1k, w1v, w2k, w2v)


NSA_TQ = 128
NSA_TK = 512
SEL_SHIFT = 6
SEL_FORCED = 1e30
SEL_FUTURE = -1e30
SEL_TAKEN = -3e38
SEL_MASK = -1e9


def _nsa_prep_kernel(ks_ref, vs_ref, kw_ref, vw_ref, kaug_ref, vst_ref, kwb_ref, vwt_ref):
    TK, TQ = NSA_TK, NSA_TQ
    k0 = pl.program_id(2) * TK
    krow = lax.broadcasted_iota(jnp.int32, (TK, LANES), 0)
    bcol = lax.broadcasted_iota(jnp.int32, (TK, LANES), 1)
    onehot = jnp.where(((k0 + krow) >> SEL_SHIFT) == bcol, 1.0, 0.0).astype(BF16)
    kaug_ref[...] = jnp.concatenate([ks_ref[...].astype(BF16), onehot], axis=1)
    vst_ref[...] = vs_ref[...].T.astype(BF16)
    kwb_ref[...] = kw_ref[...].astype(BF16)
    for h in range(TK // TQ):
        vwt_ref[h] = vw_ref[h * TQ:(h + 1) * TQ, :].T.astype(BF16)


def nsa_prep(y3):
    batch, seq, _ = y3.shape
    G, d, TK, TQ = NSA_KV, NSA_DH, NSA_TK, NSA_TQ
    nk = seq // TK
    src = lambda col: pl.BlockSpec((None, TK, d), lambda b, g, c: (b, c, col + g))
    return pl.pallas_call(
        _nsa_prep_kernel,
        out_shape=(jax.ShapeDtypeStruct((batch, G, nk, TK, d + LANES), BF16),
                   jax.ShapeDtypeStruct((batch, G, nk, d, TK), BF16),
                   jax.ShapeDtypeStruct((batch, G, seq, d), BF16),
                   jax.ShapeDtypeStruct((batch, G, seq // TQ, d, TQ), BF16)),
        grid=(batch, G, nk),
        in_specs=[src(CD_COL[n] // d) for n in ("nks", "nvs", "nkw", "nvw")],
        out_specs=(pl.BlockSpec((None, None, None, TK, d + LANES), lambda b, g, c: (b, g, c, 0, 0)),
                   pl.BlockSpec((None, None, None, d, TK), lambda b, g, c: (b, g, c, 0, 0)),
                   pl.BlockSpec((None, None, TK, d), lambda b, g, c: (b, g, c, 0)),
                   pl.BlockSpec((None, None, TK // TQ, d, TQ), lambda b, g, c: (b, g, c, 0, 0))),
        compiler_params=_params(("parallel", "parallel", "parallel")),
        name="nsa_prep",
    )(y3, y3, y3, y3)


def _nsa_attn_kernel(q_ref, gt_ref, kc_ref, vct_ref, kaug_ref, vst_ref, kwb_ref, vwt_ref, ovt_ref,
                     o_ref, acc_ref, *, seq):
    TQ, TK, d = NSA_TQ, NSA_TK, NSA_DH
    G = NSA_KV
    R = NSA_HEADS // G
    NQ = R * TQ
    groups = range(G)
    i = pl.program_id(1)
    t0 = i * TQ
    scale = d ** -0.5
    tcol = t0 + (lax.broadcasted_iota(jnp.int32, (1, NQ), 1) & (TQ - 1))

    q = q_ref[...]
    qf = [jnp.concatenate([q[:, (g * R + r) * d:(g * R + r + 1) * d] for r in range(R)], axis=0)
          for g in groups]
    qs = [x.astype(BF16) for x in qf]
    qs2 = [(x * (scale * LOG2E)).astype(BF16) for x in qf]

    ncp = kc_ref.shape[0]
    nid = lax.broadcasted_iota(jnp.int32, (ncp, NQ), 0)
    cmp_ok = nid * NSA_CMP_STRIDE + (NSA_CMP_LEN - 1) <= tcol
    s1 = [lax.dot_general(kc_ref[:, g * d:(g + 1) * d].astype(BF16), qs[g], NT_DIMS,
                          preferred_element_type=F32) * scale for g in groups]
    p1 = []
    for g in groups:
        s = jnp.where(cmp_ok, s1[g], -jnp.inf)
        m1 = jnp.max(s, axis=0, keepdims=True)
        m1 = jnp.where(m1 > -jnp.inf, m1, 0.0)
        e1 = jnp.exp(s - m1)
        z1 = jnp.sum(e1, axis=0, keepdims=True)
        p1.append(e1 / jnp.where(z1 > 0, z1, 1.0))
    o_cmp = [jnp.dot(vct_ref[g * d:(g + 1) * d, :].astype(BF16), p1[g].astype(BF16),
                     preferred_element_type=F32) for g in groups]

    W = NSA_WINDOW
    span = min(W + TQ, seq)
    kstart = pl.multiple_of(jnp.maximum(t0 + TQ - span, 0), TQ)
    kb = kstart // TQ
    rel = tcol - kstart - lax.broadcasted_iota(jnp.int32, (span, NQ), 0)
    win_ok = (rel >= 0) & (rel < W)
    s3 = [lax.dot_general(kwb_ref[g, pl.ds(kstart, span), :], qs2[g], NT_DIMS,
                          preferred_element_type=F32) for g in groups]
    o_win = []
    for g in groups:
        s = jnp.where(win_ok, s3[g], -jnp.inf)
        e3 = jnp.exp2(s - jnp.max(s, axis=0, keepdims=True))
        z3 = jnp.sum(e3, axis=0, keepdims=True)
        vwt = jnp.concatenate([vwt_ref[g, kb + h] for h in range(span // TQ)], axis=1)
        o_win.append(jnp.dot(vwt, e3.astype(BF16), preferred_element_type=F32) / z3)

    nblk = seq // NSA_SEL_LEN
    blk = lax.broadcasted_iota(jnp.int32, (LANES, TQ), 0)
    cur = (t0 + lax.broadcasted_iota(jnp.int32, (LANES, TQ), 1)) >> SEL_SHIFT
    forced = (blk == 0) | (blk == cur) | (blk == cur - 1)
    imp = []
    for g in groups:
        p1sum = p1[g][:, 0:TQ]
        for r in range(1, R):
            p1sum = p1sum + p1[g][:, r * TQ:(r + 1) * TQ]
        v = _dot_exact01(ovt_ref[...], p1sum)
        v = jnp.where(blk > cur, SEL_FUTURE, v)
        v = jnp.where(forced, SEL_FORCED, v)
        imp.append(jnp.where(blk >= nblk, SEL_TAKEN, v))
    sel = [jnp.zeros((LANES, TQ), F32) for _ in groups]
    for _ in range(min(NSA_N_SEL, nblk)):
        for g in groups:
            mx = jnp.max(imp[g], axis=0, keepdims=True)
            first = jnp.min(jnp.where(imp[g] == mx, blk, LANES), axis=0, keepdims=True)
            hit = blk == first
            sel[g] = jnp.where(hit, 1.0, sel[g])
            imp[g] = jnp.where(hit, SEL_TAKEN, imp[g])
    q_aug = []
    for g in groups:
        bias = jnp.where(sel[g] > 0, 0.0, SEL_MASK).T.astype(BF16)
        q_aug.append(jnp.concatenate([qs2[g], jnp.concatenate([bias] * R, axis=0)], axis=1))

    HC = NQ // 2
    krow = lax.broadcasted_iota(jnp.int32, (TK, HC), 0)
    acc_ref[...] = jnp.zeros_like(acc_ref)

    chains = [(g, slice(hh * HC, (hh + 1) * HC)) for g in groups for hh in range(2)]

    def sel_tiles(js, ml, masked):
        s = [[lax.dot_general(kaug_ref[g, j], q_aug[g][cols], NT_DIMS, preferred_element_type=F32)
              for g, cols in chains] for j in js]
        ml = list(ml)
        for t, j in enumerate(js):
            for c, (g, cols) in enumerate(chains):
                m_old, l_old = ml[2 * c], ml[2 * c + 1]
                sc = jnp.where(j * TK + krow <= tcol[:, cols], s[t][c], -jnp.inf) if masked else s[t][c]
                m_new = jnp.maximum(m_old, jnp.max(sc, axis=0, keepdims=True))
                alpha = jnp.exp2(m_old - m_new)
                p = jnp.exp2(sc - m_new)
                ml[2 * c] = m_new
                ml[2 * c + 1] = alpha * l_old + jnp.sum(p, axis=0, keepdims=True)
                acc_ref[c] = alpha * acc_ref[c] + jnp.dot(vst_ref[g, j], p.astype(BF16),
                                                          preferred_element_type=F32)
        return tuple(ml)

    n_full = (t0 + TQ + TK - 1) // TK - 1
    ml0 = (jnp.full((1, HC), -jnp.inf, F32), jnp.zeros((1, HC), F32)) * len(chains)
    ml = lax.fori_loop(0, n_full // 2, lambda u, c: sel_tiles((2 * u, 2 * u + 1), c, False), ml0)
    ml = lax.cond(n_full % 2 == 1, lambda c: sel_tiles((n_full - 1,), c, False), lambda c: c, ml)
    ml = sel_tiles((n_full,), ml, True)
    o_slc = [jnp.concatenate([acc_ref[2 * g] / ml[4 * g + 1], acc_ref[2 * g + 1] / ml[4 * g + 3]], axis=1)
             for g in groups]

    gt = _sigmoid(gt_ref[...]).T
    for g in groups:
        for r in range(R):
            c = 3 * (g * R + r)
            cols = slice(r * TQ, (r + 1) * TQ)
            o = (gt[c:c + 1] * o_cmp[g][:, cols] + gt[c + 1:c + 2] * o_slc[g][:, cols]
                 + gt[c + 2:c + 3] * o_win[g][:, cols])
            o_ref[:, (g * R + r) * d:(g * R + r + 1) * d] = o.T.astype(o_ref.dtype)


def nsa_attention(y3, k_cmp, v_cmp):
    batch, seq, n = y3.shape
    m = batch * seq
    TQ, TK, d = NSA_TQ, NSA_TK, NSA_DH
    R = NSA_HEADS // NSA_KV
    nq = seq // TQ
    ncp = k_cmp.shape[1]
    assert seq // NSA_SEL_LEN <= LANES and seq % TK == 0
    y2 = y3.reshape(m, n)
    v_cmp_t = v_cmp.transpose(0, 2, 1)
    nc = (seq - NSA_CMP_LEN) // NSA_CMP_STRIDE + 1
    cs = jnp.arange(ncp) * NSA_CMP_STRIDE
    ss = jnp.arange(LANES) * NSA_SEL_LEN
    overlap_t = ((cs[None, :] < ss[:, None] + NSA_SEL_LEN) & (cs[None, :] + NSA_CMP_LEN > ss[:, None])
                 & (jnp.arange(ncp)[None, :] < nc) & (jnp.arange(LANES)[:, None] < seq // NSA_SEL_LEN))
    overlap_t = overlap_t.astype(BF16)
    G = NSA_KV
    nk = seq // TK
    kaug, vst, kwb, vwt = nsa_prep(y3)
    whole = lambda shape: pl.BlockSpec((None,) + shape, lambda b, i: (b,) + (0,) * len(shape))
    return pl.pallas_call(
        functools.partial(_nsa_attn_kernel, seq=seq),
        out_shape=jax.ShapeDtypeStruct((m, MIX_HALF), BF16),
        grid=(batch, nq),
        in_specs=[
            pl.BlockSpec((TQ, NSA_HEADS * d), lambda b, i: (b * nq + i, 0)),
            pl.BlockSpec((TQ, LANES), lambda b, i: (b * nq + i, CD_GATE_COL // LANES)),
            whole((ncp, G * d)), whole((G * d, ncp)),
            whole((G, nk, TK, d + LANES)), whole((G, nk, d, TK)),
            whole((G, seq, d)), whole((G, seq // TQ, d, TQ)),
            pl.BlockSpec((LANES, ncp), lambda b, i: (0, 0)),
        ],
        out_specs=pl.BlockSpec((TQ, NSA_HEADS * d), lambda b, i: (b * nq + i, 0)),
        scratch_shapes=[pltpu.VMEM((2 * G, d, R * TQ // 2), F32)],
        compiler_params=_params(("parallel", "arbitrary")),
        name="nsa_attention",
    )(y2, y2, k_cmp, v_cmp_t, kaug, vst, kwb, vwt, overlap_t)


def _route_rows(x, nw, w2, wh, bias):
    xn = x * lax.rsqrt(jnp.mean(x * x, axis=-1, keepdims=True) + RMS_EPS) * nw
    xh = xn.astype(BF16)
    xl = (xn - xh.astype(F32)).astype(BF16)
    r1 = jnp.dot(xh, w2, preferred_element_type=F32)
    r2 = jnp.dot(xl, wh, preferred_element_type=F32)
    logits = r1[:, 0:LANES] + r1[:, LANES:] + r2 + bias
    lane = lax.broadcasted_iota(jnp.int32, logits.shape, 1)
    first_of = lambda hit: jnp.min(jnp.where(hit, lane, LANES), axis=-1, keepdims=True)
    gl = jnp.where(lane < MOE_GROUPS, logits, -jnp.inf)
    gmax = jnp.max(gl, axis=-1, keepdims=True)
    p_grp = 1.0 / jnp.sum(jnp.exp(gl - gmax), axis=-1, keepdims=True)
    g_idx = first_of(gl == gmax)
    e_lane = lane - MOE_GROUPS
    in_g = (e_lane >= 0) & (e_lane < MOE_EXPERTS) & ((e_lane >> 3) == g_idx)
    wl = jnp.where(in_g, logits, -jnp.inf)
    we = jnp.exp(wl - jnp.max(wl, axis=-1, keepdims=True))
    prob = jnp.where(in_g, we / jnp.sum(we, axis=-1, keepdims=True), -1.0)
    p0 = jnp.max(prob, axis=-1, keepdims=True)
    i0 = first_of(prob == p0)
    prob1 = jnp.where(lane == i0, -1.0, prob)
    p1 = jnp.max(prob1, axis=-1, keepdims=True)
    i1 = first_of(prob1 == p1)
    denom = p0 + p1
    vals = [(i0 - MOE_GROUPS).astype(F32), (i1 - MOE_GROUPS).astype(F32),
            p_grp * p0 / denom, p_grp * p1 / denom]
    out = jnp.zeros_like(logits)
    for c, v in enumerate(vals):
        out = jnp.where(lane == c, v, out)
    return out


def router_operands(nw, w_grp, b_grp, w_exp, b_exp):
    d = w_grp.shape[0]
    assert MOE_EPG == 8
    wr = jnp.zeros((d, LANES), F32).at[:, 0:MOE_GROUPS].set(w_grp)
    wr = wr.at[:, MOE_GROUPS:MOE_GROUPS + MOE_EXPERTS].set(w_exp)
    bias = jnp.zeros((1, LANES), F32).at[0, 0:MOE_GROUPS].set(b_grp.astype(F32))
    bias = bias.at[0, MOE_GROUPS:MOE_GROUPS + MOE_EXPERTS].set(b_exp.astype(F32))
    wh = wr.astype(BF16)
    wl = (wr - wh.astype(F32)).astype(BF16)
    return nw.reshape(1, d), jnp.concatenate([wh, wl], axis=1), wh, bias


def _cast3_kernel(a_ref, b_ref, c_ref, oa_ref, ob_ref, oc_ref):
    oa_ref[...] = a_ref[...].astype(oa_ref.dtype)
    ob_ref[...] = b_ref[...].astype(ob_ref.dtype)
    oc_ref[...] = c_ref[...].astype(oc_ref.dtype)


def expert_weights_bf16(w_gate, w_up, w_down, layer):
    _, E, d, dff = w_gate.shape
    split = 2
    in_spec = lambda r, c: pl.BlockSpec((None, None, r // split, c), lambda e, s: (layer, e, s, 0))
    out_spec = lambda r, c: pl.BlockSpec((None, r // split, c), lambda e, s: (e, s, 0))
    return pl.pallas_call(
        _cast3_kernel,
        out_shape=(jax.ShapeDtypeStruct((E, d, dff), BF16), jax.ShapeDtypeStruct((E, d, dff), BF16),
                   jax.ShapeDtypeStruct((E, dff, d), BF16)),
        grid=(E, split),
        in_specs=[in_spec(d, dff), in_spec(d, dff), in_spec(dff, d)],
        out_specs=(out_spec(d, dff), out_spec(d, dff), out_spec(dff, d)),
        compiler_params=_params(("parallel", "parallel")),
        name="expert_weights_bf16",
    )(w_gate, w_up, w_down)


SUBLANES = 8


def _gather_rows(idx_ref, base, src_ref, dst_ref, sem):
    def body(it, carry):
        for u in range(SUBLANES):
            tok = idx_ref[base + it * SUBLANES + u]
            pltpu.make_async_copy(src_ref.at[pl.ds(tok, 1)], dst_ref.at[it, pl.ds(u, 1)], sem).start()
        return carry
    lax.fori_loop(0, dst_ref.shape[0], body, 0)


def _wait_rows(dst_ref, sem):
    pltpu.make_async_copy(dst_ref, dst_ref, sem).wait()


def _expert_kernel(be_ref, nused_ref, tok_ref, h_ref, nw_ref, wg_ref, wu_ref, wd_ref, o_ref,
                   xbuf, sems):
    TM = MOE_TM
    i = pl.program_id(0)
    n_used = nused_ref[0]
    slot = i % 2

    @pl.when(i == 0)
    def _():
        _gather_rows(tok_ref, 0, h_ref, xbuf.at[0], sems.at[0])

    @pl.when(i + 1 < n_used)
    def _():
        _gather_rows(tok_ref, (i + 1) * TM, h_ref, xbuf.at[1 - slot], sems.at[1 - slot])

    @pl.when(i < n_used)
    def _():
        _wait_rows(xbuf.at[slot], sems.at[slot])
        x = xbuf[slot].reshape(TM, xbuf.shape[-1])
        xn = (x * lax.rsqrt(jnp.mean(x * x, axis=-1, keepdims=True) + RMS_EPS) * nw_ref[...]).astype(BF16)
        hg = jnp.dot(xn, wg_ref[...], preferred_element_type=F32)
        hu = jnp.dot(xn, wu_ref[...], preferred_element_type=F32)
        hb = (_silu(hg) * hu).astype(BF16)
        o_ref[...] = jnp.dot(hb, wd_ref[...], preferred_element_type=F32)

    @pl.when(i >= n_used)
    def _():
        o_ref[...] = jnp.zeros_like(o_ref)


def expert_ffn(h, nw, row_tok, blk_expert, n_used, w_gate, w_up, w_down):
    m, d = h.shape
    rows = row_tok.shape[0]
    TM = MOE_TM
    nb = rows // TM
    dff = w_gate.shape[2]
    grid_spec = pltpu.PrefetchScalarGridSpec(
        num_scalar_prefetch=3,
        grid=(nb,),
        in_specs=[
            pl.BlockSpec(memory_space=pl.ANY),
            pl.BlockSpec((1, d), lambda i, be, nu, tok: (0, 0)),
            pl.BlockSpec((None, d, dff), lambda i, be, nu, tok: (be[i], 0, 0)),
            pl.BlockSpec((None, d, dff), lambda i, be, nu, tok: (be[i], 0, 0)),
            pl.BlockSpec((None, dff, d), lambda i, be, nu, tok: (be[i], 0, 0)),
        ],
        out_specs=pl.BlockSpec((TM, d), lambda i, be, nu, tok: (i, 0)),
        scratch_shapes=[pltpu.VMEM((2, TM // SUBLANES, SUBLANES, d), F32), pltpu.SemaphoreType.DMA((2,))],
    )
    return pl.pallas_call(
        _expert_kernel,
        out_shape=jax.ShapeDtypeStruct((rows, d), F32),
        grid_spec=grid_spec,
        compiler_params=_params(("arbitrary",)),
        name="expert_ffn",
    )(blk_expert, n_used, row_tok, h, nw.reshape(1, d), w_gate, w_up, w_down)


def _combine_kernel(dest_ref, ys_ref, h_ref, gate_ref, fw_ref, o_ref, ybuf, sems, *, final_norm):
    tc = h_ref.shape[0]
    i = pl.program_id(0)
    n = pl.num_programs(0)
    slot = i % 2

    @pl.when(i == 0)
    def _():
        _gather_rows(dest_ref, 0, ys_ref, ybuf.at[0], sems.at[0])

    @pl.when(i + 1 < n)
    def _():
        _gather_rows(dest_ref, (i + 1) * 2 * tc, ys_ref, ybuf.at[1 - slot], sems.at[1 - slot])

    _wait_rows(ybuf.at[slot], sems.at[slot])
    gate = gate_ref[...]
    d = h_ref.shape[1]
    g8 = tc // SUBLANES
    y0 = ybuf[slot, 0:g8].reshape(tc, d)
    y1 = ybuf[slot, g8:].reshape(tc, d)
    out = h_ref[...] + (y0 * gate[:, 0:1] + y1 * gate[:, 1:2])
    if final_norm:
        out = out * lax.rsqrt(jnp.mean(out * out, axis=-1, keepdims=True) + RMS_EPS) * fw_ref[...]
    o_ref[...] = out


def moe_combine(h, ys, dest_tiles, gate, final_w, tc, final_norm):
    m, d = h.shape
    grid_spec = pltpu.PrefetchScalarGridSpec(
        num_scalar_prefetch=1,
        grid=(m // tc,),
        in_specs=[
            pl.BlockSpec(memory_space=pl.ANY),
            pl.BlockSpec((tc, d), lambda i, dst: (i, 0)),
            pl.BlockSpec((tc, MOE_TOPK), lambda i, dst: (i, 0)),
            pl.BlockSpec((1, d), lambda i, dst: (0, 0)),
        ],
        out_specs=pl.BlockSpec((tc, d), lambda i, dst: (i, 0)),
        scratch_shapes=[pltpu.VMEM((2, 2 * tc // SUBLANES, SUBLANES, d), F32),
                        pltpu.SemaphoreType.DMA((2,))],
    )
    return pl.pallas_call(
        functools.partial(_combine_kernel, final_norm=final_norm),
        out_shape=jax.ShapeDtypeStruct((m, d), F32),
        grid_spec=grid_spec,
        compiler_params=_params(("arbitrary",)),
        name="moe_combine",
    )(dest_tiles, ys, h, gate, final_w.reshape(1, d))


def _dispatch_tables(expert):
    n = expert.shape[0]
    K, E, TM = MOE_TOPK, MOE_EXPERTS, MOE_TM
    A = n * K
    flat_e = expert.reshape(A)
    iota = jnp.arange(A, dtype=jnp.int32)
    e_sorted, order = lax.sort((flat_e, iota), num_keys=1)
    counts = jnp.sum((flat_e[:, None] == jnp.arange(E, dtype=jnp.int32)[None, :]).astype(jnp.int32), axis=0)
    padded = (counts + TM - 1) // TM * TM
    pad_end = jnp.cumsum(padded)
    pad_start = pad_end - padded
    start = jnp.cumsum(counts) - counts
    dest_sorted = (pad_start[e_sorted] + iota - start[e_sorted]).astype(jnp.int32)
    _, dest = lax.sort((order, dest_sorted), num_keys=1)
    rows = A + E * TM
    nb = rows // TM
    blk_expert = jnp.minimum(jnp.sum((jnp.arange(nb, dtype=jnp.int32)[:, None] * TM >= pad_end[None, :])
                                     .astype(jnp.int32), axis=1), E - 1).astype(jnp.int32)
    r = jnp.arange(rows, dtype=jnp.int32)
    e_row = jnp.repeat(blk_expert, TM)
    pos = r - pad_start[e_row].astype(jnp.int32)
    src = jnp.clip(start[e_row].astype(jnp.int32) + pos, 0, A - 1)
    row_tok = jnp.where(pos < counts[e_row], order[src] // K, 0).astype(jnp.int32)
    n_used = (pad_end[-1] // TM).astype(jnp.int32).reshape(1)
    return dest.reshape(n, K), row_tok, blk_expert, n_used


def hier_moe(h, routed, nw, w_gate_all, w_up_all, w_down_all, layer, final_w, final_norm):
    m, d = h.shape
    tc = 256
    expert = routed[:, 0:MOE_TOPK].astype(jnp.int32)
    gate = routed[:, MOE_TOPK:2 * MOE_TOPK]
    dest, row_tok, blk_expert, n_used = _dispatch_tables(expert)
    w_gate, w_up, w_down = expert_weights_bf16(w_gate_all, w_up_all, w_down_all, layer)
    ys = expert_ffn(h, nw, row_tok, blk_expert, n_used, w_gate, w_up, w_down)
    dest_tiles = dest.reshape(m // tc, tc, MOE_TOPK).transpose(0, 2, 1).reshape(-1)
    return moe_combine(h, ys, dest_tiles, gate, final_w, tc, final_norm)


def _cd_in_weights(w):
    g0, g1 = CD_SRC["gt"], CD_SRC["rq"]
    gates = w[:, g0:g1]
    pad = jnp.zeros((w.shape[0], CD_N - CD_GATE_COL - gates.shape[1]), w.dtype)
    return jnp.concatenate([w[:, 0:g0], w[:, g1:], gates, pad], axis=1)


def mixer_ab(h, nw, w_in, w_out, lb, hg_norm, sw_sinks, router_ops, batch, seq):
    y = norm_matmul(h, nw, w_in.astype(BF16), min(1024, h.shape[0]), 1408)
    o_a = hgrn2(y, lb, hg_norm, batch, seq, min(512, seq))
    o_b = swa(y, sw_sinks, batch, seq)
    return outproj_residual(o_a, o_b, w_out.astype(BF16), h, router_ops, 512)


def mixer_cd(h, nw, w_in, w_out, pe_k, w1_k, w2_k, pe_v, w1_v, w2_v, ret_norm, router_ops, batch, seq):
    y = norm_matmul(h, nw, _cd_in_weights(w_in).astype(BF16), min(1024, h.shape[0]), 1152)
    y3 = y.reshape(batch, seq, CD_N)
    k_cmp, v_cmp = nsa_compress(y3, pe_k, w1_k, w2_k, pe_v, w1_v, w2_v)
    o_c = nsa_attention(y3, k_cmp, v_cmp)
    o_d = retention(y, ret_norm, batch, seq, min(512, seq))
    return outproj_residual(o_c, o_d, w_out.astype(BF16), h, router_ops, 512)


def kernel(x, norm_mix, norm_ffn, norm_final, ab_w_in, ab_w_out, hg_lb_logits, hg_norm, sw_sinks, cd_w_in, cd_w_out, nsa_pe_k, nsa_w1_k, nsa_w2_k, nsa_pe_v, nsa_w1_v, nsa_w2_v, ret_norm, moe_w_grp, moe_b_grp, moe_w_exp, moe_b_exp, moe_w_gate, moe_w_up, moe_w_down):
    batch, seq, d = x.shape
    depth = norm_mix.shape[0]
    lb_all = jnp.cumsum(jax.nn.softmax(hg_lb_logits.astype(F32), axis=0), axis=0)
    h = x.reshape(batch * seq, d)
    for layer in range(depth):
        router_ops = router_operands(norm_ffn[layer], moe_w_grp[layer], moe_b_grp[layer],
                                     moe_w_exp[layer], moe_b_exp[layer])
        if layer % 2 == 0:
            e = layer // 2
            h, routed = mixer_ab(h, norm_mix[layer], ab_w_in[e], ab_w_out[e], lb_all[layer], hg_norm[e],
                                 sw_sinks[e], router_ops, batch, seq)
        else:
            o = layer // 2
            h, routed = mixer_cd(h, norm_mix[layer], cd_w_in[o], cd_w_out[o], nsa_pe_k[o], nsa_w1_k[o],
                                 nsa_w2_k[o], nsa_pe_v[o], nsa_w1_v[o], nsa_w2_v[o], ret_norm[o],
                                 router_ops, batch, seq)
        h = hier_moe(h, routed, norm_ffn[layer], moe_w_gate, moe_w_up, moe_w_down, layer,
                     norm_final, layer == depth - 1)
    return h.reshape(batch, seq, d)
```
